```python
import jax, jax.numpy as jnp
from jax import lax
import numpy as np

D_MODEL = 4096
BATCH = 1
SEQ = 16384
DEPTH = 4

EPS = 1e-6
PLE_DIM = 256
N_BRANCH = 4
BRANCH_W = D_MODEL // N_BRANCH
POOL_WINDOWS = (2, 4, 8, 16)
POOL_GROUP = BRANCH_W // len(POOL_WINDOWS)
HEAD_DIM = 128
N_ATT_HEADS = BRANCH_W // HEAD_DIM
Q_BLOCK = 128
CONV_WIDTH = 31
SGU_CHUNK = 128
SGU_GROUPS = 8
SGU_GROUP_W = BRANCH_W // SGU_GROUPS
D_FF = ((8 * D_MODEL + 3 * 256 - 1) // (3 * 256)) * 256
GATE_RANK = 256
OFF_POOL = 0
OFF_Q = OFF_POOL + BRANCH_W
OFF_K = OFF_Q + BRANCH_W
OFF_V = OFF_K + BRANCH_W
OFF_F = OFF_V + BRANCH_W
OFF_GLU = OFF_F + N_ATT_HEADS
OFF_SGU = OFF_GLU + 2 * BRANCH_W
IN_W = OFF_SGU + 2 * BRANCH_W

kernel_name = 'hybrid_pool_fox_conv_sgu_block'


def rms_norm(x, g):
    xf = x.astype(jnp.float32)
    y = xf * lax.rsqrt(jnp.mean(xf * xf, axis=-1, keepdims=True) + EPS)
    return (y * g.astype(jnp.float32)).astype(x.dtype)


def layer_norm(x, g, b):
    xf = x.astype(jnp.float32)
    mu = jnp.mean(xf, axis=-1, keepdims=True)
    var = jnp.mean(jnp.square(xf - mu), axis=-1, keepdims=True)
    y = (xf - mu) * lax.rsqrt(var + EPS)
    return (y * g.astype(jnp.float32) + b.astype(jnp.float32)).astype(x.dtype)


def causal_window_mean(xg, w):
    S = xg.shape[1]
    cs = jnp.cumsum(xg.astype(jnp.float32), axis=1)
    prev = jnp.pad(cs, ((0, 0), (w, 0), (0, 0)))[:, :S]
    cnt = jnp.minimum(jnp.arange(1, S + 1), w).astype(jnp.float32)
    return ((cs - prev) / cnt[None, :, None]).astype(xg.dtype)


def pool_mixer(xa, pool_w, pool_scale):
    B, S, _ = xa.shape
    xg = xa.reshape(B, S, len(POOL_WINDOWS), POOL_GROUP)
    pooled = jnp.stack([causal_window_mean(xg[:, :, g], w) for g, w in enumerate(POOL_WINDOWS)], axis=2) - xg
    mixed = jnp.einsum('bsgc,gcd->bsgd', pooled, pool_w).reshape(B, S, BRANCH_W)
    return mixed * pool_scale


def fox_attention(q, k, v, fcum):
    B, S, H, Dh = q.shape
    n_blocks = S // Q_BLOCK
    scale = HEAD_DIM ** -0.5
    kpos = jnp.arange(S)

    def block(i):
        start = i * Q_BLOCK
        qb = lax.dynamic_slice_in_dim(q, start, Q_BLOCK, axis=1)
        fq = lax.dynamic_slice_in_dim(fcum, start, Q_BLOCK, axis=2)
        s = jnp.einsum('bqhd,bkhd->bhqk', qb, k).astype(jnp.float32) * scale
        s = s + fq[..., :, None] - fcum[..., None, :]
        qpos = start + jnp.arange(Q_BLOCK)
        mask = kpos[None, :] <= qpos[:, None]
        s = jnp.where(mask[None, None], s, -jnp.inf)
        pr = jax.nn.softmax(s, axis=-1)
        return jnp.einsum('bhqk,bkhd->bqhd', pr.astype(v.dtype), v)

    out = lax.map(block, jnp.arange(n_blocks))
    return jnp.moveaxis(out, 0, 1).reshape(B, S, H * Dh)


def conformer_conv(xc, conv_w, conv_b, ln_g, ln_b):
    a, g = jnp.split(xc, 2, axis=-1)
    u = a * jax.nn.sigmoid(g)
    y = lax.conv_general_dilated(u, conv_w[:, None, :], window_strides=(1,),
                                 padding=[(CONV_WIDTH - 1, 0)],
                                 dimension_numbers=('NWC', 'WIO', 'NWC'),
                                 feature_group_count=BRANCH_W) + conv_b
    return jax.nn.silu(layer_norm(y, ln_g, ln_b))


def spatial_gating(xs, w_s, b_s, ln_g, ln_b):
    z = jax.nn.gelu(xs)
    u, v = jnp.split(z, 2, axis=-1)
    v = layer_norm(v, ln_g, ln_b)
    B, S, _ = v.shape
    vr = v.reshape(B, S // SGU_CHUNK, SGU_CHUNK, SGU_GROUPS, SGU_GROUP_W)
    tril = jnp.tril(jnp.ones((SGU_CHUNK, SGU_CHUNK), dtype=w_s.dtype))
    mixed = jnp.einsum('gts,bnsgc->bntgc', w_s * tril[None], vr) + b_s.T[None, None, :, :, None]
    return u * mixed.reshape(B, S, BRANCH_W)


def setup_inputs(seed: int = 0) -> dict:
    key = jax.random.key(seed)
    ks = jax.random.split(key, 32)
    f32 = jnp.float32

    def nrm(k, shape, scale):
        return jax.random.normal(k, shape, f32) * scale

    L = DEPTH
    b_f = (jnp.linspace(1.0, 6.0, N_ATT_HEADS, dtype=f32)[None, :]
           + nrm(ks[3], (L, N_ATT_HEADS), 0.1))
    return {
        'x': nrm(ks[0], (BATCH, SEQ, D_MODEL), 1.0),
        'p': nrm(ks[1], (DEPTH, BATCH, SEQ, PLE_DIM), 1.0),
        'norm_mix': 1.0 + nrm(ks[2], (L, D_MODEL), 0.02),
        'w_in': nrm(ks[4], (L, D_MODEL, IN_W), D_MODEL ** -0.5),
        'b_f': b_f,
        'pool_w': nrm(ks[5], (L, len(POOL_WINDOWS), POOL_GROUP, POOL_GROUP), POOL_GROUP ** -0.5),
        'pool_scale': 1.0 + nrm(ks[6], (L, BRANCH_W), 0.1),
        'conv_w': nrm(ks[7], (L, CONV_WIDTH, BRANCH_W), CONV_WIDTH ** -0.5),
        'conv_b': nrm(ks[8], (L, BRANCH_W), 0.02),
        'conv_ln_g': 1.0 + nrm(ks[9], (L, BRANCH_W), 0.02),
        'conv_ln_b': nrm(ks[10], (L, BRANCH_W), 0.02),
        'sgu_ln_g': 1.0 + nrm(ks[11], (L, BRANCH_W), 0.02),
        'sgu_ln_b': nrm(ks[12], (L, BRANCH_W), 0.02),
        'sgu_w': nrm(ks[13], (L, SGU_GROUPS, SGU_CHUNK, SGU_CHUNK), SGU_CHUNK ** -0.5),
        'sgu_b': 1.0 + nrm(ks[14], (L, SGU_GROUPS, SGU_CHUNK), 0.1),
        'w_branch': nrm(ks[15], (L, N_BRANCH, BRANCH_W, D_MODEL), BRANCH_W ** -0.5),
        'w_gate_down': nrm(ks[16], (L, D_MODEL, GATE_RANK), D_MODEL ** -0.5),
        'w_gate_up': nrm(ks[17], (L, N_BRANCH, GATE_RANK, D_MODEL), GATE_RANK ** -0.5),
        'b_gate': nrm(ks[18], (L, N_BRANCH, D_MODEL), 0.02),
        'w_out': nrm(ks[19], (L, D_MODEL, D_MODEL), D_MODEL ** -0.5),
        'norm_ffn': 1.0 + nrm(ks[20], (L, D_MODEL), 0.02),
        'w_ffn_gate': nrm(ks[21], (L, D_MODEL, D_FF), D_MODEL ** -0.5),
        'w_ffn_up': nrm(ks[22], (L, D_MODEL, D_FF), D_MODEL ** -0.5),
        'w_ffn_down': nrm(ks[23], (L, D_FF, D_MODEL), D_FF ** -0.5),
        'norm_ple': 1.0 + nrm(ks[24], (L, D_MODEL), 0.02),
        'w_ple_gate_down': nrm(ks[25], (L, D_MODEL, GATE_RANK), D_MODEL ** -0.5),
        'w_ple_gate_up': nrm(ks[26], (L, GATE_RANK, D_MODEL), GATE_RANK ** -0.5),
        'w_ple': nrm(ks[27], (L, PLE_DIM, D_MODEL), PLE_DIM ** -0.5),
        'norm_final': 1.0 + nrm(ks[28], (D_MODEL,), 0.02),
    }


def reference(x, p, norm_mix, w_in, b_f, pool_w, pool_scale, conv_w, conv_b, conv_ln_g, conv_ln_b,
              sgu_ln_g, sgu_ln_b, sgu_w, sgu_b, w_branch, w_gate_down, w_gate_up, b_gate, w_out,
              norm_ffn, w_ffn_gate, w_ffn_up, w_ffn_down, norm_ple, w_ple_gate_down, w_ple_gate_up,
              w_ple, norm_final):
    B, S, _ = x.shape
    h = x
    for i in range(DEPTH):
        xn = rms_norm(h, norm_mix[i])
        proj = xn @ w_in[i]
        xa = proj[..., OFF_POOL:OFF_Q]
        q = proj[..., OFF_Q:OFF_K].reshape(B, S, N_ATT_HEADS, HEAD_DIM)
        k = proj[..., OFF_K:OFF_V].reshape(B, S, N_ATT_HEADS, HEAD_DIM)
        v = proj[..., OFF_V:OFF_F].reshape(B, S, N_ATT_HEADS, HEAD_DIM)
        f_logit = proj[..., OFF_F:OFF_GLU].astype(jnp.float32) + b_f[i].astype(jnp.float32)
        fcum = jnp.cumsum(jax.nn.log_sigmoid(f_logit), axis=1).transpose(0, 2, 1)
        xc = proj[..., OFF_GLU:OFF_SGU]
        xs = proj[..., OFF_SGU:IN_W]

        branches = (
            pool_mixer(xa, pool_w[i], pool_scale[i]),
            fox_attention(q, k, v, fcum).astype(h.dtype),
            conformer_conv(xc, conv_w[i], conv_b[i], conv_ln_g[i], conv_ln_b[i]),
            spatial_gating(xs, sgu_w[i], sgu_b[i], sgu_ln_g[i], sgu_ln_b[i]),
        )
        z = xn @ w_gate_down[i]
        merged = jnp.zeros_like(h)
        for bidx in range(N_BRANCH):
            gate = jax.nn.sigmoid(z @ w_gate_up[i, bidx] + b_gate[i, bidx])
            merged = merged + gate * (branches[bidx] @ w_branch[i, bidx])
        h = h + merged @ w_out[i]

        hn = rms_norm(h, norm_ffn[i])
        h = h + (jax.nn.silu(hn @ w_ffn_gate[i]) * (hn @ w_ffn_up[i])) @ w_ffn_down[i]

        hp = rms_norm(h, norm_ple[i])
        ple_gate = jax.nn.sigmoid((hp @ w_ple_gate_down[i]) @ w_ple_gate_up[i])
        h = h + ple_gate * (p[i] @ w_ple[i])
    return rms_norm(h, norm_final)
```

```python
import functools
import math

import jax
import jax.numpy as jnp
from jax import lax
from jax.experimental import pallas as pl
from jax.experimental.pallas import tpu as pltpu

F32 = jnp.float32
BF16 = jnp.bfloat16

EPS = 1e-6
HEAD_DIM = 128
POOL_WINDOWS = (2, 4, 8, 16)
GELU_C0 = math.sqrt(2.0 / math.pi)
GELU_C1 = 0.044715

LANES = 128
BF16_SUBLANES = 16
MIB = 1024 * 1024

POOL_HALO = 16
CONV_HALO = 32


def _params(semantics, vmem_mib):
    return pltpu.CompilerParams(dimension_semantics=semantics, vmem_limit_bytes=vmem_mib * MIB)


def _blk(n, pref):
    b = min(n, pref)
    while n % b:
        b -= LANES
    assert b > 0
    return b


def _row_scale(ssq, d):
    return lax.rsqrt(ssq / d + EPS)


def _dot(a, b):
    return jnp.dot(a, b, preferred_element_type=F32)


def _prep_kernel(x_ref, g_ref, hg_ref, ssq_ref):
    x = x_ref[...]
    ssq_ref[...] = jnp.sum(x * x, axis=-1, keepdims=True)
    hg_ref[...] = (x * g_ref[...]).astype(BF16)


def _norm_prep(x, g):
    s, d = x.shape
    bm = _blk(s, 256)
    return pl.pallas_call(
        _prep_kernel,
        grid=(s // bm,),
        in_specs=[pl.BlockSpec((bm, d), lambda i: (i, 0)), pl.BlockSpec((1, d), lambda i: (0, 0))],
        out_specs=[pl.BlockSpec((bm, d), lambda i: (i, 0)), pl.BlockSpec((bm, 1), lambda i: (i, 0))],
        out_shape=[jax.ShapeDtypeStruct((s, d), BF16), jax.ShapeDtypeStruct((s, 1), F32)],
        compiler_params=_params(("parallel",), 32),
        name="norm_prep",
    )(x, g.reshape(1, d))


def _final_kernel(h_ref, ssq_ref, g_ref, o_ref, *, d):
    y = h_ref[...] * _row_scale(ssq_ref[...], d)
    o_ref[...] = y * g_ref[...]


def _final_norm(h, ssq, g):
    s, d = h.shape
    bm = _blk(s, 256)
    return pl.pallas_call(
        functools.partial(_final_kernel, d=d),
        grid=(s // bm,),
        in_specs=[pl.BlockSpec((bm, d), lambda i: (i, 0)), pl.BlockSpec((bm, 1), lambda i: (i, 0)),
                  pl.BlockSpec((1, d), lambda i: (0, 0))],
        out_specs=pl.BlockSpec((bm, d), lambda i: (i, 0)),
        out_shape=jax.ShapeDtypeStruct((s, d), F32),
        compiler_params=_params(("parallel",), 32),
        name="final_norm",
    )(h, ssq, g.reshape(1, d))


def _in_kernel(x_ref, ssq_ref, w_ref, o_ref, *, d):
    acc = _dot(x_ref[...], w_ref[...])
    o_ref[...] = (acc * _row_scale(ssq_ref[...], d)).astype(o_ref.dtype)


def _in_proj(hg, ssq, w):
    s, d = hg.shape
    n = w.shape[1]
    bm, bn = _blk(s, 1024), _blk(n, 1024)
    return pl.pallas_call(
        functools.partial(_in_kernel, d=d),
        grid=(s // bm, n // bn),
        in_specs=[pl.BlockSpec((bm, d), lambda i, j: (i, 0)), pl.BlockSpec((bm, 1), lambda i, j: (i, 0)),
                  pl.BlockSpec((d, bn), lambda i, j: (0, j))],
        out_specs=pl.BlockSpec((bm, bn), lambda i, j: (i, j)),
        out_shape=jax.ShapeDtypeStruct((s, n), BF16),
        compiler_params=_params(("parallel", "parallel"), 56),
        name="in_proj",
    )(hg, ssq, w)


def _tail_kernel(x_ref, ssq_ref, w_ref, bf_ref, z_ref, f_ref, carry_ref, *, d, rank, bm):
    i = pl.program_id(0)

    @pl.when(i == 0)
    def _():
        carry_ref[...] = jnp.zeros_like(carry_ref)

    acc = _dot(x_ref[...], w_ref[...]) * _row_scale(ssq_ref[...], d)
    z_ref[...] = acc[:, :rank].astype(BF16)
    logit = acc[:, rank:] + bf_ref[...]
    logf = jnp.minimum(logit, 0.0) - jnp.log1p(jnp.exp(-jnp.abs(logit)))
    row = lax.broadcasted_iota(jnp.int32, (bm, bm), 0)
    col = lax.broadcasted_iota(jnp.int32, (bm, bm), 1)
    tri = (col <= row).astype(BF16)
    hi = logf.astype(BF16)
    rem = logf - hi.astype(F32)
    mid = rem.astype(BF16)
    lo = (rem - mid.astype(F32)).astype(BF16)
    cs = _dot(tri, hi) + _dot(tri, mid) + _dot(tri, lo) + carry_ref[...]
    f_ref[...] = cs
    carry_ref[...] = cs[bm - 1:bm, :]


def _gate_tail(hg, ssq, w_tail, bf_pad, rank):
    s, d = hg.shape
    n = w_tail.shape[1]
    bm = _blk(s, 512)
    return pl.pallas_call(
        functools.partial(_tail_kernel, d=d, rank=rank, bm=bm),
        grid=(s // bm,),
        in_specs=[pl.BlockSpec((bm, d), lambda i: (i, 0)), pl.BlockSpec((bm, 1), lambda i: (i, 0)),
                  pl.BlockSpec((d, n), lambda i: (0, 0)), pl.BlockSpec((1, LANES), lambda i: (0, 0))],
        out_specs=[pl.BlockSpec((bm, rank), lambda i: (i, 0)), pl.BlockSpec((bm, LANES), lambda i: (i, 0))],
        out_shape=[jax.ShapeDtypeStruct((s, rank), BF16), jax.ShapeDtypeStruct((s, LANES), F32)],
        scratch_shapes=[pltpu.VMEM((1, LANES), F32)],
        compiler_params=_params(("arbitrary",), 40),
        name="gate_tail",
    )(hg, ssq, w_tail, bf_pad)


def _pool_kernel(x_ref, halo_ref, w_ref, sc_ref, o_ref, *, bm, gw):
    i = pl.program_id(0)
    x = x_ref[...].astype(F32)
    halo = jnp.where(i > 0, halo_ref[...].astype(F32), 0.0)
    xx = jnp.concatenate([halo, x], axis=0)
    pos = i * bm + lax.broadcasted_iota(jnp.int32, (bm, 1), 0)
    for g, w in enumerate(POOL_WINDOWS):
        sl = slice(g * gw, (g + 1) * gw)
        s = xx[:, sl]
        span = 1
        while span < w:
            s = s + pltpu.roll(s, span, axis=0)
            span *= 2
        cnt = jnp.minimum(pos + 1, w).astype(F32)
        pooled = s[POOL_HALO:, :] / cnt - x[:, sl]
        mixed = _dot(pooled.astype(BF16), w_ref[g])
        o_ref[:, sl] = (mixed * sc_ref[:, sl]).astype(BF16)


def _pool_mixer(proj, col_blk, pool_w, pool_scale):
    s = proj.shape[0]
    ng, gw, _ = pool_w.shape
    c = ng * gw
    assert all(w & (w - 1) == 0 and w - 1 <= POOL_HALO for w in POOL_WINDOWS) and ng == len(POOL_WINDOWS)
    bm = _blk(s, 512)
    per = bm // POOL_HALO
    return pl.pallas_call(
        functools.partial(_pool_kernel, bm=bm, gw=gw),
        grid=(s // bm,),
        in_specs=[pl.BlockSpec((bm, c), lambda i: (i, col_blk)),
                  pl.BlockSpec((POOL_HALO, c), lambda i: (jnp.maximum(i * per - 1, 0), col_blk)),
                  pl.BlockSpec((ng, gw, gw), lambda i: (0, 0, 0)),
                  pl.BlockSpec((1, c), lambda i: (0, 0))],
        out_specs=pl.BlockSpec((bm, c), lambda i: (i, 0)),
        out_shape=jax.ShapeDtypeStruct((s, c), BF16),
        compiler_params=_params(("parallel",), 32),
        name="pool_mixer",
    )(proj, proj, pool_w, pool_scale.reshape(1, c))


def _attn_kernel(q_ref, k_ref, v_ref, fq_ref, fk_ref, o_ref, *, blk, scale):
    i = pl.program_id(1)
    q = q_ref[...]
    fq = fq_ref[...]

    def step(j, carry, masked):
        m, l, acc = carry
        off = pl.multiple_of(j * blk, blk)
        k = k_ref[pl.ds(off, blk), :]
        v = v_ref[pl.ds(off, blk), :]
        s = lax.dot_general(q, k, (((1,), (1,)), ((), ())), preferred_element_type=F32)
        s = s * scale + fq - fk_ref[j]
        if masked:
            row = lax.broadcasted_iota(jnp.int32, (blk, blk), 0)
            col = lax.broadcasted_iota(jnp.int32, (blk, blk), 1)
            s = jnp.where(col <= row, s, -jnp.inf)
        m_new = jnp.maximum(m, jnp.max(s, axis=-1, keepdims=True))
        alpha = jnp.exp(m - m_new)
        p = jnp.exp(s - m_new)
        l = alpha * l + jnp.sum(p, axis=-1, keepdims=True)
        acc = alpha * acc + _dot(p.astype(BF16), v)
        return m_new, l, acc

    init = (jnp.full((blk, 1), -jnp.inf, F32), jnp.zeros((blk, 1), F32), jnp.zeros((blk, HEAD_DIM), F32))
    carry = lax.fori_loop(0, i, lambda j, c: step(j, c, False), init)
    _, l, acc = step(i, carry, True)
    o_ref[...] = (acc / l).astype(o_ref.dtype)


def _fox_attention(proj, q_blk, k_blk, v_blk, fcum, n_heads):
    s = proj.shape[0]
    blk = _blk(s, 512)
    nb = s // blk
    fq = fcum.T.reshape(n_heads, s, 1)
    fk = fcum.T.reshape(n_heads, nb, 1, blk)
    return pl.pallas_call(
        functools.partial(_attn_kernel, blk=blk, scale=HEAD_DIM ** -0.5),
        grid=(n_heads, nb),
        in_specs=[pl.BlockSpec((blk, HEAD_DIM), lambda h, i: (i, q_blk + h)),
                  pl.BlockSpec((s, HEAD_DIM), lambda h, i: (0, k_blk + h)),
                  pl.BlockSpec((s, HEAD_DIM), lambda h, i: (0, v_blk + h)),
                  pl.BlockSpec((None, blk, 1), lambda h, i: (h, i, 0)),
                  pl.BlockSpec((None, nb, 1, blk), lambda h, i: (h, 0, 0, 0))],
        out_specs=pl.BlockSpec((blk, HEAD_DIM), lambda h, i: (i, h)),
        out_shape=jax.ShapeDtypeStruct((s, n_heads * HEAD_DIM), BF16),
        compiler_params=_params(("parallel", "parallel"), 48),
        name="fox_attention",
    )(proj, proj, proj, fq, fk)


def _layer_norm(x, g, b):
    mu = jnp.mean(x, axis=-1, keepdims=True)
    xc = x - mu
    var = jnp.mean(xc * xc, axis=-1, keepdims=True)
    return xc * lax.rsqrt(var + EPS) * g + b


def _conv_kernel(a_ref, g_ref, ah_ref, gh_ref, cw_ref, cb_ref, lg_ref, lb_ref, o_ref, u_scr, *, bm, kw):
    i = pl.program_id(0)
    u_scr[CONV_HALO:, :] = a_ref[...].astype(F32) * jax.nn.sigmoid(g_ref[...].astype(F32))
    uh = ah_ref[...].astype(F32) * jax.nn.sigmoid(gh_ref[...].astype(F32))
    u_scr[:CONV_HALO, :] = jnp.where(i > 0, uh, 0.0)
    y = jnp.zeros(o_ref.shape, F32) + cb_ref[...]
    for j in range(kw):
        y = y + cw_ref[j:j + 1, :] * u_scr[pl.ds(CONV_HALO - (kw - 1) + j, bm), :]
    y = _layer_norm(y, lg_ref[...], lb_ref[...])
    o_ref[...] = (y * jax.nn.sigmoid(y)).astype(BF16)


def _conformer_conv(proj, a_blk, conv_w, conv_b, ln_g, ln_b):
    s = proj.shape[0]
    kw, c = conv_w.shape
    assert kw - 1 <= CONV_HALO
    bm = _blk(s, 256)
    per = bm // CONV_HALO
    halo_row = lambda i: jnp.maximum(i * per - 1, 0)
    vec = lambda: pl.BlockSpec((1, c), lambda i: (0, 0))
    return pl.pallas_call(
        functools.partial(_conv_kernel, bm=bm, kw=kw),
        grid=(s // bm,),
        in_specs=[pl.BlockSpec((bm, c), lambda i: (i, a_blk)),
                  pl.BlockSpec((bm, c), lambda i: (i, a_blk + 1)),
                  pl.BlockSpec((CONV_HALO, c), lambda i: (halo_row(i), a_blk)),
                  pl.BlockSpec((CONV_HALO, c), lambda i: (halo_row(i), a_blk + 1)),
                  pl.BlockSpec((kw, c), lambda i: (0, 0)), vec(), vec(), vec()],
        out_specs=pl.BlockSpec((bm, c), lambda i: (i, 0)),
        out_shape=jax.ShapeDtypeStruct((s, c), BF16),
        scratch_shapes=[pltpu.VMEM((bm + CONV_HALO, c), F32)],
        compiler_params=_params(("parallel",), 32),
        name="conformer_conv",
    )(proj, proj, proj, proj, conv_w, conv_b.reshape(1, c), ln_g.reshape(1, c), ln_b.reshape(1, c))


def _gelu_tanh(x):
    return x * (0.5 * (1.0 + jnp.tanh(GELU_C0 * (x + GELU_C1 * (x * x * x)))))


def _sgu_kernel(u_ref, v_ref, w_ref, bt_ref, lg_ref, lb_ref, o_ref, *, bm, chunk, gw):
    u = _gelu_tanh(u_ref[...].astype(F32))
    v = _layer_norm(_gelu_tanh(v_ref[...].astype(F32)), lg_ref[...], lb_ref[...]).astype(BF16)
    row = lax.broadcasted_iota(jnp.int32, (chunk, chunk), 0)
    col = lax.broadcasted_iota(jnp.int32, (chunk, chunk), 1)
    for g in range(w_ref.shape[0]):
        wg = jnp.where(col <= row, w_ref[g], 0.0).astype(BF16)
        bias = bt_ref[:, g:g + 1]
        cs = slice(g * gw, (g + 1) * gw)
        for n in range(bm // chunk):
            rs = slice(n * chunk, (n + 1) * chunk)
            mixed = _dot(wg, v[rs, cs]) + bias
            o_ref[rs, cs] = (u[rs, cs] * mixed).astype(BF16)


def _spatial_gating(proj, u_blk, w_s, b_s, ln_g, ln_b):
    s = proj.shape[0]
    ng, chunk, _ = w_s.shape
    c = ln_g.shape[0]
    gw = c // ng
    bm = _blk(s, 4 * chunk)
    vec = lambda: pl.BlockSpec((1, c), lambda i: (0, 0))
    return pl.pallas_call(
        functools.partial(_sgu_kernel, bm=bm, chunk=chunk, gw=gw),
        grid=(s // bm,),
        in_specs=[pl.BlockSpec((bm, c), lambda i: (i, u_blk)),
                  pl.BlockSpec((bm, c), lambda i: (i, u_blk + 1)),
                  pl.BlockSpec((ng, chunk, chunk), lambda i: (0, 0, 0)),
                  pl.BlockSpec((chunk, ng), lambda i: (0, 0)), vec(), vec()],
        out_specs=pl.BlockSpec((bm, c), lambda i: (i, 0)),
        out_shape=jax.ShapeDtypeStruct((s, c), BF16),
        compiler_params=_params(("parallel",), 32),
        name="spatial_gating",
    )(proj, proj, w_s, b_s.T, ln_g.reshape(1, c), ln_b.reshape(1, c))


def _merge_kernel(b0_ref, b1_ref, b2_ref, b3_ref, z_ref, wb_ref, wg_ref, bg_ref, o_ref):
    z = z_ref[...]
    merged = None
    for b, br in enumerate((b0_ref, b1_ref, b2_ref, b3_ref)):
        gate = jax.nn.sigmoid(_dot(z, wg_ref[b]) + bg_ref[b:b + 1, :])
        term = gate * _dot(br[...], wb_ref[b])
        merged = term if merged is None else merged + term
    o_ref[...] = merged.astype(BF16)


def _merge(branches, z, w_branch, w_gate_up, b_gate):
    s, c = branches[0].shape
    nb, rank, d = w_gate_up.shape
    assert nb == 4 and len(branches) == 4
    bm, bn = _blk(s, 512), _blk(d, 1024)
    act = lambda w: pl.BlockSpec((bm, w), lambda j, i: (i, 0))
    return pl.pallas_call(
        _merge_kernel,
        grid=(d // bn, s // bm),
        in_specs=[act(c), act(c), act(c), act(c), act(rank),
                  pl.BlockSpec((nb, c, bn), lambda j, i: (0, 0, j)),
                  pl.BlockSpec((nb, rank, bn), lambda j, i: (0, 0, j)),
                  pl.BlockSpec((nb, bn), lambda j, i: (0, j))],
        out_specs=pl.BlockSpec((bm, bn), lambda j, i: (i, j)),
        out_shape=jax.ShapeDtypeStruct((s, d), BF16),
        compiler_params=_params(("parallel", "parallel"), 56),
        name="branch_merge",
    )(*branches, z, w_branch, w_gate_up, b_gate)


def _emit_resid(hn, j, g_ref, ho_ref, hg_ref, ssq_ref):
    ho_ref[...] = hn
    hg_ref[...] = (hn * g_ref[...]).astype(BF16)
    part = jnp.sum(hn * hn, axis=-1, keepdims=True)

    @pl.when(j == 0)
    def _():
        ssq_ref[...] = part

    @pl.when(j > 0)
    def _():
        ssq_ref[...] += part


def _resid_kernel(x_ref, w_ref, h_ref, g_ref, ho_ref, hg_ref, ssq_ref, *scratch, nk):
    j, k = pl.program_id(1), pl.program_id(2)
    part = _dot(x_ref[...], w_ref[...])
    if nk == 1:
        _emit_resid(h_ref[...] + part, j, g_ref, ho_ref, hg_ref, ssq_ref)
        return
    acc_ref, = scratch

    @pl.when(k == 0)
    def _():
        acc_ref[...] = part

    @pl.when(k > 0)
    def _():
        acc_ref[...] += part

    @pl.when(k == nk - 1)
    def _():
        _emit_resid(h_ref[...] + acc_ref[...], j, g_ref, ho_ref, hg_ref, ssq_ref)


def _resid_matmul(x, w, h, g, bk_pref):
    s, kdim = x.shape
    d = w.shape[1]
    bm, bn, bk = _blk(s, 1024), _blk(d, 1024), _blk(kdim, bk_pref)
    nk = kdim // bk
    return pl.pallas_call(
        functools.partial(_resid_kernel, nk=nk),
        grid=(s // bm, d // bn, nk),
        in_specs=[pl.BlockSpec((bm, bk), lambda i, j, k: (i, k)),
                  pl.BlockSpec((bk, bn), lambda i, j, k: (k, j)),
                  pl.BlockSpec((bm, bn), lambda i, j, k: (i, j)),
                  pl.BlockSpec((1, bn), lambda i, j, k: (0, j))],
        out_specs=[pl.BlockSpec((bm, bn), lambda i, j, k: (i, j)),
                   pl.BlockSpec((bm, bn), lambda i, j, k: (i, j)),
                   pl.BlockSpec((bm, 1), lambda i, j, k: (i, 0))],
        out_shape=[jax.ShapeDtypeStruct((s, d), F32), jax.ShapeDtypeStruct((s, d), BF16),
                   jax.ShapeDtypeStruct((s, 1), F32)],
        scratch_shapes=[pltpu.VMEM((bm, bn), F32)] if nk > 1 else [],
        compiler_params=_params(("parallel", "arbitrary", "arbitrary"), 56),
        name="resid_matmul",
    )(x, w, h, g.reshape(1, d))


def _ffn_act_kernel(x_ref, ssq_ref, wg_ref, wu_ref, o_ref, *, d):
    r = _row_scale(ssq_ref[...], d)
    x = x_ref[...]
    a = _dot(x, wg_ref[...]) * r
    u = _dot(x, wu_ref[...]) * r
    o_ref[...] = (a * jax.nn.sigmoid(a) * u).astype(BF16)


def _ffn_act(hg, ssq, w_gate, w_up):
    s, d = hg.shape
    f = w_gate.shape[1]
    bm, bn = _blk(s, 1024), _blk(f, 512)
    return pl.pallas_call(
        functools.partial(_ffn_act_kernel, d=d),
        grid=(s // bm, f // bn),
        in_specs=[pl.BlockSpec((bm, d), lambda i, j: (i, 0)), pl.BlockSpec((bm, 1), lambda i, j: (i, 0)),
                  pl.BlockSpec((d, bn), lambda i, j: (0, j)), pl.BlockSpec((d, bn), lambda i, j: (0, j))],
        out_specs=pl.BlockSpec((bm, bn), lambda i, j: (i, j)),
        out_shape=jax.ShapeDtypeStruct((s, f), BF16),
        compiler_params=_params(("parallel", "parallel"), 56),
        name="ffn_act",
    )(hg, ssq, w_gate, w_up)


def _ple_kernel(hg_ref, ssq_ref, wd_ref, wu_ref, p_ref, wp_ref, h_ref, g_ref,
                ho_ref, xg_ref, ssqo_ref, t_ref, *, d):
    j = pl.program_id(1)

    @pl.when(j == 0)
    def _():
        t = _dot(hg_ref[...], wd_ref[...]) * _row_scale(ssq_ref[...], d)
        t_ref[...] = t.astype(BF16)

    gate = jax.nn.sigmoid(_dot(t_ref[...], wu_ref[...]))
    inj = _dot(p_ref[...].astype(BF16), wp_ref[...])
    _emit_resid(h_ref[...] + gate * inj, j, g_ref, ho_ref, xg_ref, ssqo_ref)


def _ple(hg, ssq, w_down, w_up, p, w_ple, h, g_next):
    s, d = hg.shape
    rank = w_down.shape[1]
    pdim = p.shape[1]
    bm, bn = _blk(s, 512), _blk(d, 1024)
    return pl.pallas_call(
        functools.partial(_ple_kernel, d=d),
        grid=(s // bm, d // bn),
        in_specs=[pl.BlockSpec((bm, d), lambda i, j: (i, 0)), pl.BlockSpec((bm, 1), lambda i, j: (i, 0)),
                  pl.BlockSpec((d, rank), lambda i, j: (0, 0)),
                  pl.BlockSpec((rank, bn), lambda i, j: (0, j)),
                  pl.BlockSpec((bm, pdim), lambda i, j: (i, 0)),
                  pl.BlockSpec((pdim, bn), lambda i, j: (0, j)),
                  pl.BlockSpec((bm, bn), lambda i, j: (i, j)),
                  pl.BlockSpec((1, bn), lambda i, j: (0, j))],
        out_specs=[pl.BlockSpec((bm, bn), lambda i, j: (i, j)),
                   pl.BlockSpec((bm, bn), lambda i, j: (i, j)),
                   pl.BlockSpec((bm, 1), lambda i, j: (i, 0))],
        out_shape=[jax.ShapeDtypeStruct((s, d), F32), jax.ShapeDtypeStruct((s, d), BF16),
                   jax.ShapeDtypeStruct((s, 1), F32)],
        scratch_shapes=[pltpu.VMEM((bm, rank), BF16)],
        compiler_params=_params(("parallel", "arbitrary"), 48),
        name="per_layer_input",
    )(hg, ssq, w_down, w_up, p, w_ple, h, g_next.reshape(1, d))


def _pad_to(a, axis, mult):
    pad = (-a.shape[axis]) % mult
    if pad == 0:
        return a
    widths = [(0, 0)] * a.ndim
    widths[axis] = (0, pad)
    return jnp.pad(a, widths)


def kernel(x, p, norm_mix, w_in, b_f, pool_w, pool_scale, conv_w, conv_b, conv_ln_g, conv_ln_b, sgu_ln_g, sgu_ln_b, sgu_w, sgu_b, w_branch, w_gate_down, w_gate_up, b_gate, w_out, norm_ffn, w_ffn_gate, w_ffn_up, w_ffn_down, norm_ple, w_ple_gate_down, w_ple_gate_up, w_ple, norm_final):
    batch, s, d = x.shape
    assert batch == 1
    depth = w_in.shape[0]
    c = pool_scale.shape[1]
    n_heads = b_f.shape[1]
    rank = w_gate_down.shape[2]
    assert c == n_heads * HEAD_DIM and w_in.shape[2] == 8 * c + n_heads
    off_f = 4 * c

    h = x.reshape(s, d)
    hg, ssq = _norm_prep(h, norm_mix[0])
    for i in range(depth):
        w_main = jnp.concatenate([w_in[i, :, :off_f], w_in[i, :, off_f + n_heads:]], axis=1).astype(BF16)
        w_tail = jnp.concatenate(
            [w_gate_down[i], _pad_to(w_in[i, :, off_f:off_f + n_heads], 1, LANES)], axis=1).astype(BF16)
        bf_pad = _pad_to(b_f[i].reshape(1, n_heads), 1, LANES)

        proj = _in_proj(hg, ssq, w_main)
        z, fcum = _gate_tail(hg, ssq, w_tail, bf_pad, rank)
        hpc = c // HEAD_DIM
        branches = (
            _pool_mixer(proj, 0, pool_w[i].astype(BF16), pool_scale[i]),
            _fox_attention(proj, hpc, 2 * hpc, 3 * hpc, fcum[:, :n_heads], n_heads),
            _conformer_conv(proj, 4, conv_w[i], conv_b[i], conv_ln_g[i], conv_ln_b[i]),
            _spatial_gating(proj, 6, sgu_w[i], sgu_b[i], sgu_ln_g[i], sgu_ln_b[i]),
        )
        merged = _merge(branches, z, w_branch[i].astype(BF16), w_gate_up[i].astype(BF16), b_gate[i])
        h, hg, ssq = _resid_matmul(merged, w_out[i].astype(BF16), h, norm_ffn[i], 2048)

        act = _ffn_act(hg, ssq, _pad_to(w_ffn_gate[i].astype(BF16), 1, 1024),
                       _pad_to(w_ffn_up[i].astype(BF16), 1, 1024))
        h, hg, ssq = _resid_matmul(act, _pad_to(w_ffn_down[i].astype(BF16), 0, 1024), h, norm_ple[i], 1024)

        g_next = norm_mix[i + 1] if i + 1 < depth else norm_final
        h, hg, ssq = _ple(hg, ssq, w_ple_gate_down[i].astype(BF16), w_ple_gate_up[i].astype(BF16),
                          p[i, 0], w_ple[i].astype(BF16), h, g_next)
    return _final_norm(h, ssq, norm_final).reshape(batch, s, d)
```

```python
import functools
import math

import jax
import jax.numpy as jnp
from jax import lax
from jax.experimental import pallas as pl
from jax.experimental.pallas import tpu as pltpu

F32 = jnp.float32
BF16 = jnp.bfloat16

EPS = 1e-6
HEAD_DIM = 128
POOL_WINDOWS = (2, 4, 8, 16)
GELU_C0 = math.sqrt(2.0 / math.pi)
GELU_C1 = 0.044715
LOG2E = math.log2(math.e)

LANES = 128
MXU_DEPTH = 256
MIB = 1024 * 1024

POOL_HALO = 16
CONV_HALO = 32
N_BIAS_PIECES = 3


def _params(semantics, vmem_mib):
    return pltpu.CompilerParams(dimension_semantics=semantics, vmem_limit_bytes=vmem_mib * MIB)


def _blk(n, pref):
    b = min(n, pref)
    while n % b:
        b -= LANES
    assert b > 0
    return b


def _row_scale(ssq, d):
    return lax.rsqrt(ssq / d + EPS)


def _dot(a, b):
    return jnp.dot(a, b, preferred_element_type=F32)


def _split3(x):
    hi = x.astype(BF16)
    rem = x - hi.astype(F32)
    mid = rem.astype(BF16)
    lo = (rem - mid.astype(F32)).astype(BF16)
    return hi, mid, lo


def _prep_kernel(x_ref, g_ref, hg_ref, ssq_ref):
    x = x_ref[...]
    ssq_ref[...] = jnp.sum(x * x, axis=-1, keepdims=True)
    hg_ref[...] = (x * g_ref[...]).astype(BF16)


def _norm_prep(x, g):
    s, d = x.shape
    bm = _blk(s, 256)
    return pl.pallas_call(
        _prep_kernel,
        grid=(s // bm,),
        in_specs=[pl.BlockSpec((bm, d), lambda i: (i, 0)), pl.BlockSpec((1, d), lambda i: (0, 0))],
        out_specs=[pl.BlockSpec((bm, d), lambda i: (i, 0)), pl.BlockSpec((bm, 1), lambda i: (i, 0))],
        out_shape=[jax.ShapeDtypeStruct((s, d), BF16), jax.ShapeDtypeStruct((s, 1), F32)],
        compiler_params=_params(("parallel",), 32),
        name="norm_prep",
    )(x, g.reshape(1, d))


def _final_kernel(h_ref, ssq_ref, g_ref, o_ref, *, d):
    y = h_ref[...] * _row_scale(ssq_ref[...], d)
    o_ref[...] = y * g_ref[...]


def _final_norm(h, ssq, g):
    s, d = h.shape
    bm = _blk(s, 256)
    return pl.pallas_call(
        functools.partial(_final_kernel, d=d),
        grid=(s // bm,),
        in_specs=[pl.BlockSpec((bm, d), lambda i: (i, 0)), pl.BlockSpec((bm, 1), lambda i: (i, 0)),
                  pl.BlockSpec((1, d), lambda i: (0, 0))],
        out_specs=pl.BlockSpec((bm, d), lambda i: (i, 0)),
        out_shape=jax.ShapeDtypeStruct((s, d), F32),
        compiler_params=_params(("parallel",), 32),
        name="final_norm",
    )(h, ssq, g.reshape(1, d))


def _in_kernel(x_ref, ssq_ref, w_ref, cs_ref, o_ref, *, d):
    acc = _dot(x_ref[...], w_ref[...])
    o_ref[...] = (acc * _row_scale(ssq_ref[...], d) * cs_ref[...]).astype(o_ref.dtype)


def _in_proj(hg, ssq, w, layer, col_scale):
    s, d = hg.shape
    n = w.shape[2]
    bm, bn = _blk(s, 1024), _blk(n, 1024)
    return pl.pallas_call(
        functools.partial(_in_kernel, d=d),
        grid=(s // bm, n // bn),
        in_specs=[pl.BlockSpec((bm, d), lambda i, j: (i, 0)), pl.BlockSpec((bm, 1), lambda i, j: (i, 0)),
                  pl.BlockSpec((None, d, bn), lambda i, j: (layer, 0, j)),
                  pl.BlockSpec((1, bn), lambda i, j: (0, j))],
        out_specs=pl.BlockSpec((bm, bn), lambda i, j: (i, j)),
        out_shape=jax.ShapeDtypeStruct((s, n), BF16),
        compiler_params=_params(("parallel", "parallel"), 56),
        name="in_proj",
    )(hg, ssq, w, col_scale)


def _tail_kernel(x_ref, ssq_ref, w_ref, bf_ref, z_ref, f_ref, carry_ref, *, d, rank, bm):
    i = pl.program_id(0)

    @pl.when(i == 0)
    def _():
        carry_ref[...] = jnp.zeros_like(carry_ref)

    acc = _dot(x_ref[...], w_ref[...]) * _row_scale(ssq_ref[...], d)
    z_ref[...] = acc[:, :rank].astype(BF16)
    logit = acc[:, rank:] + bf_ref[...]
    logf = jnp.minimum(logit, 0.0) - jnp.log1p(jnp.exp(-jnp.abs(logit)))
    row = lax.broadcasted_iota(jnp.int32, (bm, bm), 0)
    col = lax.broadcasted_iota(jnp.int32, (bm, bm), 1)
    tri = (col <= row).astype(BF16)
    hi, mid, lo = _split3(logf)
    cs = _dot(tri, hi) + _dot(tri, mid) + _dot(tri, lo) + carry_ref[...]
    carry_ref[...] = cs[bm - 1:bm, :]
    for n, piece in enumerate(_split3(cs * LOG2E)):
        f_ref[:, n * LANES:(n + 1) * LANES] = piece


def _gate_tail(hg, ssq, w_tail, layer, bf_pad, rank):
    s, d = hg.shape
    n = w_tail.shape[2]
    bm = _blk(s, 512)
    return pl.pallas_call(
        functools.partial(_tail_kernel, d=d, rank=rank, bm=bm),
        grid=(s // bm,),
        in_specs=[pl.BlockSpec((bm, d), lambda i: (i, 0)), pl.BlockSpec((bm, 1), lambda i: (i, 0)),
                  pl.BlockSpec((None, d, n), lambda i: (layer, 0, 0)),
                  pl.BlockSpec((None, 1, LANES), lambda i: (layer, 0, 0))],
        out_specs=[pl.BlockSpec((bm, rank), lambda i: (i, 0)),
                   pl.BlockSpec((bm, N_BIAS_PIECES * LANES), lambda i: (i, 0))],
        out_shape=[jax.ShapeDtypeStruct((s, rank), BF16),
                   jax.ShapeDtypeStruct((s, N_BIAS_PIECES * LANES), BF16)],
        scratch_shapes=[pltpu.VMEM((1, LANES), F32)],
        compiler_params=_params(("arbitrary",), 40),
        name="gate_tail",
    )(hg, ssq, w_tail, bf_pad)


def _pool_kernel(x_ref, halo_ref, w_ref, sc_ref, o_ref, *, bm, gw):
    i = pl.program_id(0)
    x = x_ref[...].astype(F32)
    halo = jnp.where(i > 0, halo_ref[...].astype(F32), 0.0)
    xx = jnp.concatenate([halo, x], axis=0)
    pos = i * bm + lax.broadcasted_iota(jnp.int32, (bm, 1), 0)
    for g, w in enumerate(POOL_WINDOWS):
        sl = slice(g * gw, (g + 1) * gw)
        s = xx[:, sl]
        span = 1
        while span < w:
            s = s + pltpu.roll(s, span, axis=0)
            span *= 2
        cnt = jnp.minimum(pos + 1, w).astype(F32)
        pooled = s[POOL_HALO:, :] / cnt - x[:, sl]
        mixed = _dot(pooled.astype(BF16), w_ref[g])
        o_ref[:, sl] = (mixed * sc_ref[:, sl]).astype(BF16)


def _pool_mixer(proj, col_blk, pool_w, layer, pool_scale):
    s = proj.shape[0]
    _, ng, gw, _ = pool_w.shape
    c = ng * gw
    assert all(w & (w - 1) == 0 and w - 1 <= POOL_HALO for w in POOL_WINDOWS) and ng == len(POOL_WINDOWS)
    bm = _blk(s, 512)
    per = bm // POOL_HALO
    return pl.pallas_call(
        functools.partial(_pool_kernel, bm=bm, gw=gw),
        grid=(s // bm,),
        in_specs=[pl.BlockSpec((bm, c), lambda i: (i, col_blk)),
                  pl.BlockSpec((POOL_HALO, c), lambda i: (jnp.maximum(i * per - 1, 0), col_blk)),
                  pl.BlockSpec((None, ng, gw, gw), lambda i: (layer, 0, 0, 0)),
                  pl.BlockSpec((None, 1, c), lambda i: (layer, 0, 0))],
        out_specs=pl.BlockSpec((bm, c), lambda i: (i, 0)),
        out_shape=jax.ShapeDtypeStruct((s, c), BF16),
        compiler_params=_params(("parallel",), 32),
        name="pool_mixer",
    )(proj, proj, pool_w, pool_scale)


def _attn_kernel(qt_ref, k_ref, vt_ref, o_ref, st0, st1, p0, p1, acc_ref, *, blk):
    i = pl.program_id(1)
    qt = qt_ref[...]

    def scores(j):
        off = pl.multiple_of(j * blk, blk)
        return _dot(k_ref[pl.ds(off, blk), :], qt)

    def softmax(st_ref, p_ref, m, l, masked):
        st = st_ref[...]
        if masked:
            key = lax.broadcasted_iota(jnp.int32, (blk, blk), 0)
            qry = lax.broadcasted_iota(jnp.int32, (blk, blk), 1)
            st = jnp.where(key <= qry, st, -jnp.inf)
        m_new = jnp.maximum(m, jnp.max(st, axis=0, keepdims=True))
        alpha = jnp.exp2(m - m_new)
        p = jnp.exp2(st - m_new)
        p_ref[...] = p.astype(BF16)
        return m_new, alpha * l + jnp.sum(p, axis=0, keepdims=True), alpha

    def pv(p_ref, j):
        return _dot(vt_ref[j], p_ref[...])

    def pair(a, m, l, last):
        st1[...] = scores(a + 1)
        owed = pv(p1, jnp.maximum(a - 1, 0))
        m, l, alpha = softmax(st0, p0, m, l, False)
        acc_ref[...] = (acc_ref[...] + owed) * alpha
        if not last:
            st0[...] = scores(a + 2)
        owed = pv(p0, a)
        m, l, alpha = softmax(st1, p1, m, l, last)
        acc_ref[...] = (acc_ref[...] + owed) * alpha
        return m, l

    p1[...] = jnp.zeros_like(p1)
    acc_ref[...] = jnp.zeros_like(acc_ref)
    st0[...] = scores(0)
    m0 = jnp.full((1, blk), -jnp.inf, F32)
    l0 = jnp.zeros((1, blk), F32)
    m, l = lax.fori_loop(0, i // 2, lambda t, c: pair(2 * t, c[0], c[1], False), (m0, l0))

    def finish(l, p_ref):
        out = (acc_ref[...] + pv(p_ref, i)) / l
        o_ref[...] = out.T.astype(o_ref.dtype)

    @pl.when(i % 2 == 1)
    def _():
        _, l_fin = pair(i - 1, m, l, True)
        finish(l_fin, p1)

    @pl.when(i % 2 == 0)
    def _():
        owed = pv(p1, jnp.maximum(i - 1, 0))
        _, l_fin, alpha = softmax(st0, p0, m, l, True)
        acc_ref[...] = (acc_ref[...] + owed) * alpha
        finish(l_fin, p0)


def _fox_attention(proj, f_pieces, n_heads):
    s = proj.shape[0]
    c = n_heads * HEAD_DIM
    blk = _blk(s, 512)
    nb = s // blk
    heads = lambda a: a.reshape(s, n_heads, HEAD_DIM)
    pieces = f_pieces.reshape(s, N_BIAS_PIECES, LANES)[:, :, :n_heads].transpose(0, 2, 1)
    ones = jnp.ones_like(pieces)
    zpad = jnp.zeros((s, n_heads, MXU_DEPTH - HEAD_DIM - 2 * N_BIAS_PIECES), BF16)
    q_aug = jnp.concatenate([heads(proj[:, c:2 * c]), pieces, ones, zpad], axis=-1)
    k_aug = jnp.concatenate([heads(proj[:, 2 * c:3 * c]), ones, -pieces, zpad], axis=-1)
    qt = q_aug.transpose(1, 2, 0)
    ka = k_aug.transpose(1, 0, 2)
    vt = proj[:, 3 * c:4 * c].reshape(nb, blk, n_heads, HEAD_DIM).transpose(2, 0, 3, 1)
    return pl.pallas_call(
        functools.partial(_attn_kernel, blk=blk),
        grid=(n_heads, nb),
        in_specs=[pl.BlockSpec((None, MXU_DEPTH, blk), lambda h, i: (h, 0, i)),
                  pl.BlockSpec((None, s, MXU_DEPTH), lambda h, i: (h, 0, 0)),
                  pl.BlockSpec((None, nb, HEAD_DIM, blk), lambda h, i: (h, 0, 0, 0))],
        out_specs=pl.BlockSpec((blk, HEAD_DIM), lambda h, i: (i, h)),
        out_shape=jax.ShapeDtypeStruct((s, c), BF16),
        scratch_shapes=[pltpu.VMEM((blk, blk), F32), pltpu.VMEM((blk, blk), F32),
                        pltpu.VMEM((blk, blk), BF16), pltpu.VMEM((blk, blk), BF16),
                        pltpu.VMEM((HEAD_DIM, blk), F32)],
        compiler_params=_params(("parallel", "parallel"), 48),
        name="fox_attention",
    )(qt, ka, vt)


def _layer_norm(x, g, b):
    mu = jnp.mean(x, axis=-1, keepdims=True)
    xc = x - mu
    var = jnp.mean(xc * xc, axis=-1, keepdims=True)
    return xc * lax.rsqrt(var + EPS) * g + b


def _conv_kernel(a_ref, g_ref, ah_ref, gh_ref, cw_ref, cb_ref, lg_ref, lb_ref, o_ref, u_scr, *, bm, kw):
    i = pl.program_id(0)
    u_scr[CONV_HALO:, :] = a_ref[...].astype(F32) * jax.nn.sigmoid(g_ref[...].astype(F32))
    uh = ah_ref[...].astype(F32) * jax.nn.sigmoid(gh_ref[...].astype(F32))
    u_scr[:CONV_HALO, :] = jnp.where(i > 0, uh, 0.0)
    y = jnp.zeros(o_ref.shape, F32) + cb_ref[...]
    for j in range(kw):
        y = y + cw_ref[j:j + 1, :] * u_scr[pl.ds(CONV_HALO - (kw - 1) + j, bm), :]
    y = _layer_norm(y, lg_ref[...], lb_ref[...])
    o_ref[...] = (y * jax.nn.sigmoid(y)).astype(BF16)


def _conformer_conv(proj, a_blk, conv_w, conv_b, ln_g, ln_b, layer):
    s = proj.shape[0]
    _, kw, c = conv_w.shape
    assert kw - 1 <= CONV_HALO
    bm = _blk(s, 256)
    per = bm // CONV_HALO
    halo_row = lambda i: jnp.maximum(i * per - 1, 0)
    vec = lambda: pl.BlockSpec((None, 1, c), lambda i: (layer, 0, 0))
    return pl.pallas_call(
        functools.partial(_conv_kernel, bm=bm, kw=kw),
        grid=(s // bm,),
        in_specs=[pl.BlockSpec((bm, c), lambda i: (i, a_blk)),
                  pl.BlockSpec((bm, c), lambda i: (i, a_blk + 1)),
                  pl.BlockSpec((CONV_HALO, c), lambda i: (halo_row(i), a_blk)),
                  pl.BlockSpec((CONV_HALO, c), lambda i: (halo_row(i), a_blk + 1)),
                  pl.BlockSpec((None, kw, c), lambda i: (layer, 0, 0)), vec(), vec(), vec()],
        out_specs=pl.BlockSpec((bm, c), lambda i: (i, 0)),
        out_shape=jax.ShapeDtypeStruct((s, c), BF16),
        scratch_shapes=[pltpu.VMEM((bm + CONV_HALO, c), F32)],
        compiler_params=_params(("parallel",), 32),
        name="conformer_conv",
    )(proj, proj, proj, proj, conv_w, conv_b, ln_g, ln_b)


def _gelu_tanh(x):
    return x * (0.5 * (1.0 + jnp.tanh(GELU_C0 * (x + GELU_C1 * (x * x * x)))))


def _sgu_kernel(u_ref, v_ref, w_ref, bt_ref, lg_ref, lb_ref, o_ref, *, bm, chunk, gw):
    u = _gelu_tanh(u_ref[...].astype(F32))
    v = _layer_norm(_gelu_tanh(v_ref[...].astype(F32)), lg_ref[...], lb_ref[...]).astype(BF16)
    row = lax.broadcasted_iota(jnp.int32, (chunk, chunk), 0)
    col = lax.broadcasted_iota(jnp.int32, (chunk, chunk), 1)
    for g in range(w_ref.shape[0]):
        wg = jnp.where(col <= row, w_ref[g], 0.0).astype(BF16)
        bias = bt_ref[:, g:g + 1]
        cs = slice(g * gw, (g + 1) * gw)
        for n in range(bm // chunk):
            rs = slice(n * chunk, (n + 1) * chunk)
            mixed = _dot(wg, v[rs, cs]) + bias
            o_ref[rs, cs] = (u[rs, cs] * mixed).astype(BF16)


def _spatial_gating(proj, u_blk, w_s, b_st, ln_g, ln_b, layer):
    s = proj.shape[0]
    _, ng, chunk, _ = w_s.shape
    c = ln_g.shape[-1]
    gw = c // ng
    bm = _blk(s, 4 * chunk)
    vec = lambda: pl.BlockSpec((None, 1, c), lambda i: (layer, 0, 0))
    return pl.pallas_call(
        functools.partial(_sgu_kernel, bm=bm, chunk=chunk, gw=gw),
        grid=(s // bm,),
        in_specs=[pl.BlockSpec((bm, c), lambda i: (i, u_blk)),
                  pl.BlockSpec((bm, c), lambda i: (i, u_blk + 1)),
                  pl.BlockSpec((None, ng, chunk, chunk), lambda i: (layer, 0, 0, 0)),
                  pl.BlockSpec((None, chunk, ng), lambda i: (layer, 0, 0)), vec(), vec()],
        out_specs=pl.BlockSpec((bm, c), lambda i: (i, 0)),
        out_shape=jax.ShapeDtypeStruct((s, c), BF16),
        compiler_params=_params(("parallel",), 32),
        name="spatial_gating",
    )(proj, proj, w_s, b_st, ln_g, ln_b)


def _merge_kernel(b0_ref, b1_ref, b2_ref, b3_ref, z_ref, wb_ref, wg_ref, bg_ref, o_ref):
    z = z_ref[...]
    merged = None
    for b, br in enumerate((b0_ref, b1_ref, b2_ref, b3_ref)):
        gate = jax.nn.sigmoid(_dot(z, wg_ref[b]) + bg_ref[b:b + 1, :])
        term = gate * _dot(br[...], wb_ref[b])
        merged = term if merged is None else merged + term
    o_ref[...] = merged.astype(BF16)


def _merge(branches, z, w_branch, w_gate_up, b_gate, layer):
    s, c = branches[0].shape
    _, nb, rank, d = w_gate_up.shape
    assert nb == 4 and len(branches) == 4
    bm, bn = _blk(s, 512), _blk(d, 1024)
    act = lambda w: pl.BlockSpec((bm, w), lambda j, i: (i, 0))
    return pl.pallas_call(
        _merge_kernel,
        grid=(d // bn, s // bm),
        in_specs=[act(c), act(c), act(c), act(c), act(rank),
                  pl.BlockSpec((None, nb, c, bn), lambda j, i: (layer, 0, 0, j)),
                  pl.BlockSpec((None, nb, rank, bn), lambda j, i: (layer, 0, 0, j)),
                  pl.BlockSpec((None, nb, bn), lambda j, i: (layer, 0, j))],
        out_specs=pl.BlockSpec((bm, bn), lambda j, i: (i, j)),
        out_shape=jax.ShapeDtypeStruct((s, d), BF16),
        compiler_params=_params(("parallel", "parallel"), 56),
        name="branch_merge",
    )(*branches, z, w_branch, w_gate_up, b_gate)


def _emit_resid(hn, j, g_ref, ho_ref, hg_ref, ssq_ref):
    ho_ref[...] = hn
    hg_ref[...] = (hn * g_ref[...]).astype(BF16)
    part = jnp.sum(hn * hn, axis=-1, keepdims=True)

    @pl.when(j == 0)
    def _():
        ssq_ref[...] = part

    @pl.when(j > 0)
    def _():
        ssq_ref[...] += part


def _resid_kernel(x_ref, w_ref, h_ref, g_ref, ho_ref, hg_ref, ssq_ref):
    hn = h_ref[...] + _dot(x_ref[...], w_ref[...])
    _emit_resid(hn, pl.program_id(1), g_ref, ho_ref, hg_ref, ssq_ref)


def _resid_matmul(x, w, layer, h, g, g_layer, bm_pref, bn_pref):
    s, kdim = x.shape
    d = w.shape[2]
    bm, bn = _blk(s, bm_pref), _blk(d, bn_pref)
    return pl.pallas_call(
        _resid_kernel,
        grid=(s // bm, d // bn),
        in_specs=[pl.BlockSpec((bm, kdim), lambda i, j: (i, 0)),
                  pl.BlockSpec((None, kdim, bn), lambda i, j: (layer, 0, j)),
                  pl.BlockSpec((bm, bn), lambda i, j: (i, j)),
                  pl.BlockSpec((None, 1, bn), lambda i, j: (g_layer, 0, j))],
        out_specs=[pl.BlockSpec((bm, bn), lambda i, j: (i, j)),
                   pl.BlockSpec((bm, bn), lambda i, j: (i, j)),
                   pl.BlockSpec((bm, 1), lambda i, j: (i, 0))],
        out_shape=[jax.ShapeDtypeStruct((s, d), F32), jax.ShapeDtypeStruct((s, d), BF16),
                   jax.ShapeDtypeStruct((s, 1), F32)],
        compiler_params=_params(("parallel", "arbitrary"), 56),
        name="resid_matmul",
    )(x, w, h, g)


def _ffn_act_kernel(x_ref, ssq_ref, wg_ref, wu_ref, o_ref, *, d):
    r = _row_scale(ssq_ref[...], d)
    x = x_ref[...]
    a = _dot(x, wg_ref[...]) * r
    u = _dot(x, wu_ref[...]) * r
    o_ref[...] = (a * jax.nn.sigmoid(a) * u).astype(BF16)


def _ffn_act(hg, ssq, w_gate, w_up, layer):
    s, d = hg.shape
    f = w_gate.shape[2]
    bm, bn = _blk(s, 1024), _blk(f, 512)
    wspec = lambda: pl.BlockSpec((None, d, bn), lambda i, j: (layer, 0, j))
    return pl.pallas_call(
        functools.partial(_ffn_act_kernel, d=d),
        grid=(s // bm, f // bn),
        in_specs=[pl.BlockSpec((bm, d), lambda i, j: (i, 0)), pl.BlockSpec((bm, 1), lambda i, j: (i, 0)),
                  wspec(), wspec()],
        out_specs=pl.BlockSpec((bm, bn), lambda i, j: (i, j)),
        out_shape=jax.ShapeDtypeStruct((s, f), BF16),
        compiler_params=_params(("parallel", "parallel"), 56),
        name="ffn_act",
    )(hg, ssq, w_gate, w_up)


def _ple_kernel(hg_ref, ssq_ref, wd_ref, wu_ref, p_ref, wp_ref, h_ref, g_ref,
                ho_ref, xg_ref, ssqo_ref, t_ref, *, d):
    j = pl.program_id(1)

    @pl.when(j == 0)
    def _():
        t = _dot(hg_ref[...], wd_ref[...]) * _row_scale(ssq_ref[...], d)
        t_ref[...] = t.astype(BF16)

    gate = jax.nn.sigmoid(_dot(t_ref[...], wu_ref[...]))
    inj = _dot(p_ref[...].astype(BF16), wp_ref[...])
    _emit_resid(h_ref[...] + gate * inj, j, g_ref, ho_ref, xg_ref, ssqo_ref)


def _ple(hg, ssq, w_down, w_up, p, w_ple, layer, h, g_next):
    s, d = hg.shape
    rank = w_down.shape[2]
    pdim = p.shape[2]
    bm, bn = _blk(s, 512), _blk(d, 1024)
    return pl.pallas_call(
        functools.partial(_ple_kernel, d=d),
        grid=(s // bm, d // bn),
        in_specs=[pl.BlockSpec((bm, d), lambda i, j: (i, 0)), pl.BlockSpec((bm, 1), lambda i, j: (i, 0)),
                  pl.BlockSpec((None, d, rank), lambda i, j: (layer, 0, 0)),
                  pl.BlockSpec((None, rank, bn), lambda i, j: (layer, 0, j)),
                  pl.BlockSpec((None, bm, pdim), lambda i, j: (layer, i, 0)),
                  pl.BlockSpec((None, pdim, bn), lambda i, j: (layer, 0, j)),
                  pl.BlockSpec((bm, bn), lambda i, j: (i, j)),
                  pl.BlockSpec((1, bn), lambda i, j: (0, j))],
        out_specs=[pl.BlockSpec((bm, bn), lambda i, j: (i, j)),
                   pl.BlockSpec((bm, bn), lambda i, j: (i, j)),
                   pl.BlockSpec((bm, 1), lambda i, j: (i, 0))],
        out_shape=[jax.ShapeDtypeStruct((s, d), F32), jax.ShapeDtypeStruct((s, d), BF16),
                   jax.ShapeDtypeStruct((s, 1), F32)],
        scratch_shapes=[pltpu.VMEM((bm, rank), BF16)],
        compiler_params=_params(("parallel", "arbitrary"), 48),
        name="per_layer_input",
    )(hg, ssq, w_down, w_up, p, w_ple, h, g_next.reshape(1, d))


def kernel(x, p, norm_mix, w_in, b_f, pool_w, pool_scale, conv_w, conv_b, conv_ln_g, conv_ln_b, sgu_ln_g, sgu_ln_b, sgu_w, sgu_b, w_branch, w_gate_down, w_gate_up, b_gate, w_out, norm_ffn, w_ffn_gate, w_ffn_up, w_ffn_down, norm_ple, w_ple_gate_down, w_ple_gate_up, w_ple, norm_final):
    batch, s, d = x.shape
    assert batch == 1
    depth = w_in.shape[0]
    c = pool_scale.shape[1]
    n_heads = b_f.shape[1]
    rank = w_gate_down.shape[2]
    assert c == n_heads * HEAD_DIM and w_in.shape[2] == 8 * c + n_heads
    off_f = 4 * c

    bf = lambda a: a.astype(BF16)
    w_main = bf(jnp.concatenate([w_in[:, :, :off_f], w_in[:, :, off_f + n_heads:]], axis=2))
    f_cols = jnp.pad(w_in[:, :, off_f:off_f + n_heads], ((0, 0), (0, 0), (0, LANES - n_heads)))
    w_tail = bf(jnp.concatenate([w_gate_down, f_cols], axis=2))
    vec3 = lambda a: a.reshape(depth, 1, a.shape[-1])
    bf_pad = vec3(jnp.pad(b_f, ((0, 0), (0, LANES - n_heads))))
    pool_scale, conv_b, conv_ln_g, conv_ln_b = vec3(pool_scale), vec3(conv_b), vec3(conv_ln_g), vec3(conv_ln_b)
    sgu_ln_g, sgu_ln_b, norm_ffn, norm_ple = vec3(sgu_ln_g), vec3(sgu_ln_b), vec3(norm_ffn), vec3(norm_ple)
    col_scale = jnp.ones((1, 8 * c), F32).at[:, c:2 * c].set(HEAD_DIM ** -0.5 * LOG2E)
    pool_wb, w_branch_b, w_gate_up_b, w_out_b = bf(pool_w), bf(w_branch), bf(w_gate_up), bf(w_out)
    w_fg, w_fu, w_fd = bf(w_ffn_gate), bf(w_ffn_up), bf(w_ffn_down)
    w_pd, w_pu, w_pl = bf(w_ple_gate_down), bf(w_ple_gate_up), bf(w_ple)
    sgu_bt = sgu_b.transpose(0, 2, 1)
    p3 = p.reshape(depth, s, p.shape[-1])

    h = x.reshape(s, d)
    hg, ssq = _norm_prep(h, norm_mix[0])
    for i in range(depth):
        proj = _in_proj(hg, ssq, w_main, i, col_scale)
        z, f_pieces = _gate_tail(hg, ssq, w_tail, i, bf_pad, rank)
        branches = (
            _pool_mixer(proj, 0, pool_wb, i, pool_scale),
            _fox_attention(proj, f_pieces, n_heads),
            _conformer_conv(proj, 4, conv_w, conv_b, conv_ln_g, conv_ln_b, i),
            _spatial_gating(proj, 6, sgu_w, sgu_bt, sgu_ln_g, sgu_ln_b, i),
        )
        merged = _merge(branches, z, w_branch_b, w_gate_up_b, b_gate, i)
        h, hg, ssq = _resid_matmul(merged, w_out_b, i, h, norm_ffn, i, 1024, 512)

        act = _ffn_act(hg, ssq, w_fg, w_fu, i)
        h, hg, ssq = _resid_matmul(act, w_fd, i, h, norm_ple, i, 512, 512)

        g_next = norm_mix[i + 1] if i + 1 < depth else norm_final
        h, hg, ssq = _ple(hg, ssq, w_pd, w_pu, p3, w_pl, i, h, g_next)
    return _final_norm(h, ssq, norm_final).reshape(batch, s, d)
```

```python
import functools
import math

import jax
import jax.numpy as jnp
from jax import lax
from jax.experimental import pallas as pl
from jax.experimental.pallas import tpu as pltpu

F32 = jnp.float32
BF16 = jnp.bfloat16

EPS = 1e-6
HEAD_DIM = 128
POOL_WINDOWS = (2, 4, 8, 16)
GELU_C0 = math.sqrt(2.0 / math.pi)
GELU_C1 = 0.044715
LOG2E = math.log2(math.e)

LANES = 128
MXU_DEPTH = 256
MIB = 1024 * 1024

POOL_HALO = 16
CONV_HALO = 32
N_BIAS_PIECES = 3
ATTN_GROUP = 4
CONV_STRIP = 128
SUBLANES = 8


def _params(semantics, vmem_mib):
    return pltpu.CompilerParams(dimension_semantics=semantics, vmem_limit_bytes=vmem_mib * MIB)


def _blk(n, pref):
    b = min(n, pref)
    while n % b:
        b -= LANES
    assert b > 0
    return b


def _row_scale(ssq, d):
    return lax.rsqrt(ssq / d + EPS)


def _dot(a, b):
    return jnp.dot(a, b, preferred_element_type=F32)


def _split3(x):
    hi = x.astype(BF16)
    rem = x - hi.astype(F32)
    mid = rem.astype(BF16)
    lo = (rem - mid.astype(F32)).astype(BF16)
    return hi, mid, lo


def _prep_kernel(x_ref, g_ref, hg_ref, ssq_ref):
    x = x_ref[...]
    ssq_ref[...] = jnp.sum(x * x, axis=-1, keepdims=True)
    hg_ref[...] = (x * g_ref[...]).astype(BF16)


def _norm_prep(x, g):
    s, d = x.shape
    bm = _blk(s, 256)
    return pl.pallas_call(
        _prep_kernel,
        grid=(s // bm,),
        in_specs=[pl.BlockSpec((bm, d), lambda i: (i, 0)), pl.BlockSpec((1, d), lambda i: (0, 0))],
        out_specs=[pl.BlockSpec((bm, d), lambda i: (i, 0)), pl.BlockSpec((bm, 1), lambda i: (i, 0))],
        out_shape=[jax.ShapeDtypeStruct((s, d), BF16), jax.ShapeDtypeStruct((s, 1), F32)],
        compiler_params=_params(("parallel",), 32),
        name="norm_prep",
    )(x, g.reshape(1, d))


def _final_kernel(h_ref, ssq_ref, g_ref, o_ref, *, d):
    y = h_ref[...] * _row_scale(ssq_ref[...], d)
    o_ref[...] = y * g_ref[...]


def _final_norm(h, ssq, g):
    s, d = h.shape
    bm = _blk(s, 256)
    return pl.pallas_call(
        functools.partial(_final_kernel, d=d),
        grid=(s // bm,),
        in_specs=[pl.BlockSpec((bm, d), lambda i: (i, 0)), pl.BlockSpec((bm, 1), lambda i: (i, 0)),
                  pl.BlockSpec((1, d), lambda i: (0, 0))],
        out_specs=pl.BlockSpec((bm, d), lambda i: (i, 0)),
        out_shape=jax.ShapeDtypeStruct((s, d), F32),
        compiler_params=_params(("parallel",), 32),
        name="final_norm",
    )(h, ssq, g.reshape(1, d))


def _in_kernel(x_ref, ssq_ref, w_ref, cs_ref, o_ref, *, d):
    acc = _dot(x_ref[...], w_ref[...])
    o_ref[...] = (acc * _row_scale(ssq_ref[...], d) * cs_ref[...]).astype(o_ref.dtype)


def _in_proj(hg, ssq, w, layer, col_scale):
    s, d = hg.shape
    n = w.shape[2]
    bm, bn = _blk(s, 1024), _blk(n, 1024)
    return pl.pallas_call(
        functools.partial(_in_kernel, d=d),
        grid=(s // bm, n // bn),
        in_specs=[pl.BlockSpec((bm, d), lambda i, j: (i, 0)), pl.BlockSpec((bm, 1), lambda i, j: (i, 0)),
                  pl.BlockSpec((None, d, bn), lambda i, j: (layer, 0, j)),
                  pl.BlockSpec((1, bn), lambda i, j: (0, j))],
        out_specs=pl.BlockSpec((bm, bn), lambda i, j: (i, j)),
        out_shape=jax.ShapeDtypeStruct((s, n), BF16),
        compiler_params=_params(("parallel", "parallel"), 56),
        name="in_proj",
    )(hg, ssq, w, col_scale)


def _tail_kernel(x_ref, ssq_ref, w_ref, bf_ref, z_ref, f_ref, carry_ref, *, d, rank, bm):
    i = pl.program_id(0)

    @pl.when(i == 0)
    def _():
        carry_ref[...] = jnp.zeros_like(carry_ref)

    acc = _dot(x_ref[...], w_ref[...]) * _row_scale(ssq_ref[...], d)
    z_ref[...] = acc[:, :rank].astype(BF16)
    logit = acc[:, rank:] + bf_ref[...]
    logf = jnp.minimum(logit, 0.0) - jnp.log1p(jnp.exp(-jnp.abs(logit)))
    row = lax.broadcasted_iota(jnp.int32, (bm, bm), 0)
    col = lax.broadcasted_iota(jnp.int32, (bm, bm), 1)
    tri = (col <= row).astype(BF16)
    hi, mid, lo = _split3(logf)
    cs = _dot(tri, hi) + _dot(tri, mid) + _dot(tri, lo) + carry_ref[...]
    carry_ref[...] = cs[bm - 1:bm, :]
    for n, piece in enumerate(_split3(cs * LOG2E)):
        f_ref[:, n * LANES:(n + 1) * LANES] = piece


def _gate_tail(hg, ssq, w_tail, layer, bf_pad, rank):
    s, d = hg.shape
    n = w_tail.shape[2]
    bm = _blk(s, 512)
    return pl.pallas_call(
        functools.partial(_tail_kernel, d=d, rank=rank, bm=bm),
        grid=(s // bm,),
        in_specs=[pl.BlockSpec((bm, d), lambda i: (i, 0)), pl.BlockSpec((bm, 1), lambda i: (i, 0)),
                  pl.BlockSpec((None, d, n), lambda i: (layer, 0, 0)),
                  pl.BlockSpec((None, 1, LANES), lambda i: (layer, 0, 0))],
        out_specs=[pl.BlockSpec((bm, rank), lambda i: (i, 0)),
                   pl.BlockSpec((bm, N_BIAS_PIECES * LANES), lambda i: (i, 0))],
        out_shape=[jax.ShapeDtypeStruct((s, rank), BF16),
                   jax.ShapeDtypeStruct((s, N_BIAS_PIECES * LANES), BF16)],
        scratch_shapes=[pltpu.VMEM((1, LANES), F32)],
        compiler_params=_params(("arbitrary",), 40),
        name="gate_tail",
    )(hg, ssq, w_tail, bf_pad)


def _pool_kernel(x_ref, halo_ref, w_ref, sc_ref, o_ref, *, bm, gw):
    i = pl.program_id(0)
    x = x_ref[...].astype(F32)
    halo = jnp.where(i > 0, halo_ref[...].astype(F32), 0.0)
    xx = jnp.concatenate([halo, x], axis=0)
    pos = i * bm + lax.broadcasted_iota(jnp.int32, (bm, 1), 0)
    for g, w in enumerate(POOL_WINDOWS):
        sl = slice(g * gw, (g + 1) * gw)
        s = xx[:, sl]
        span = 1
        while span < w:
            s = s + pltpu.roll(s, span, axis=0)
            span *= 2
        cnt = jnp.minimum(pos + 1, w).astype(F32)
        pooled = s[POOL_HALO:, :] / cnt - x[:, sl]
        mixed = _dot(pooled.astype(BF16), w_ref[g])
        o_ref[:, sl] = (mixed * sc_ref[:, sl]).astype(BF16)


def _pool_mixer(proj, col_blk, pool_w, layer, pool_scale):
    s = proj.shape[0]
    _, ng, gw, _ = pool_w.shape
    c = ng * gw
    assert all(w & (w - 1) == 0 and w - 1 <= POOL_HALO for w in POOL_WINDOWS) and ng == len(POOL_WINDOWS)
    bm = _blk(s, 512)
    per = bm // POOL_HALO
    return pl.pallas_call(
        functools.partial(_pool_kernel, bm=bm, gw=gw),
        grid=(s // bm,),
        in_specs=[pl.BlockSpec((bm, c), lambda i: (i, col_blk)),
                  pl.BlockSpec((POOL_HALO, c), lambda i: (jnp.maximum(i * per - 1, 0), col_blk)),
                  pl.BlockSpec((None, ng, gw, gw), lambda i: (layer, 0, 0, 0)),
                  pl.BlockSpec((None, 1, c), lambda i: (layer, 0, 0))],
        out_specs=pl.BlockSpec((bm, c), lambda i: (i, 0)),
        out_shape=jax.ShapeDtypeStruct((s, c), BF16),
        compiler_params=_params(("parallel",), 32),
        name="pool_mixer",
    )(proj, proj, pool_w, pool_scale)


def _attn_kernel(qt_ref, k_ref, vt_ref, o_ref, st0, st1, p0, p1, acc_ref, *, blk):
    i = pl.program_id(1)
    qt = qt_ref[...]

    def scores(j):
        off = pl.multiple_of(j * blk, blk)
        return _dot(k_ref[pl.ds(off, blk), :], qt)

    def softmax(st_ref, p_ref, m, l, masked):
        st = st_ref[...]
        if masked:
            key = lax.broadcasted_iota(jnp.int32, (blk, blk), 0)
            qry = lax.broadcasted_iota(jnp.int32, (blk, blk), 1)
            st = jnp.where(key <= qry, st, -jnp.inf)
        m_new = jnp.maximum(m, jnp.max(st, axis=0, keepdims=True))
        alpha = jnp.exp2(m - m_new)
        p = jnp.exp2(st - m_new)
        p_ref[...] = p.astype(BF16)
        return m_new, alpha * l + jnp.sum(p, axis=0, keepdims=True), alpha

    def pv(p_ref, j):
        return _dot(vt_ref[j], p_ref[...])

    st_bufs, p_bufs = (st0, st1), (p0, p1)

    def group(a, m, l, n, tail):
        for b in range(n):
            last = tail and b == n - 1
            if not last:
                st_bufs[(b + 1) % 2][...] = scores(a + b + 1)
            owed = pv(p_bufs[(b + 1) % 2], jnp.maximum(a + b - 1, 0))
            m, l, alpha = softmax(st_bufs[b % 2], p_bufs[b % 2], m, l, last)
            acc_ref[...] = (acc_ref[...] + owed) * alpha
        return m, l

    p1[...] = jnp.zeros_like(p1)
    acc_ref[...] = jnp.zeros_like(acc_ref)
    st0[...] = scores(0)
    m0 = jnp.full((1, blk), -jnp.inf, F32)
    l0 = jnp.zeros((1, blk), F32)
    trips = i // ATTN_GROUP
    m, l = lax.fori_loop(0, trips, lambda t, c: group(ATTN_GROUP * t, c[0], c[1], ATTN_GROUP, False), (m0, l0))
    rem = i + 1 - ATTN_GROUP * trips

    for n in range(1, ATTN_GROUP + 1):
        @pl.when(rem == n)
        def _(n=n):
            _, l_fin = group(i + 1 - n, m, l, n, True)
            out = (acc_ref[...] + pv(p_bufs[(n - 1) % 2], i)) / l_fin
            o_ref[...] = out.T.astype(o_ref.dtype)


def _fox_attention(proj, f_pieces, n_heads):
    s = proj.shape[0]
    c = n_heads * HEAD_DIM
    blk = _blk(s, 512)
    nb = s // blk
    heads = lambda a: a.reshape(s, n_heads, HEAD_DIM)
    pieces = f_pieces.reshape(s, N_BIAS_PIECES, LANES)[:, :, :n_heads].transpose(0, 2, 1)
    ones = jnp.ones_like(pieces)
    zpad = jnp.zeros((s, n_heads, MXU_DEPTH - HEAD_DIM - 2 * N_BIAS_PIECES), BF16)
    q_aug = jnp.concatenate([heads(proj[:, c:2 * c]), pieces, ones, zpad], axis=-1)
    k_aug = jnp.concatenate([heads(proj[:, 2 * c:3 * c]), ones, -pieces, zpad], axis=-1)
    qt = q_aug.transpose(1, 2, 0)
    ka = k_aug.transpose(1, 0, 2)
    vt = proj[:, 3 * c:4 * c].reshape(nb, blk, n_heads, HEAD_DIM).transpose(2, 0, 3, 1)
    return pl.pallas_call(
        functools.partial(_attn_kernel, blk=blk),
        grid=(n_heads, nb),
        in_specs=[pl.BlockSpec((None, MXU_DEPTH, blk), lambda h, i: (h, 0, i)),
                  pl.BlockSpec((None, s, MXU_DEPTH), lambda h, i: (h, 0, 0)),
                  pl.BlockSpec((None, nb, HEAD_DIM, blk), lambda h, i: (h, 0, 0, 0))],
        out_specs=pl.BlockSpec((blk, HEAD_DIM), lambda h, i: (i, h)),
        out_shape=jax.ShapeDtypeStruct((s, c), BF16),
        scratch_shapes=[pltpu.VMEM((blk, blk), F32), pltpu.VMEM((blk, blk), F32),
                        pltpu.VMEM((blk, blk), BF16), pltpu.VMEM((blk, blk), BF16),
                        pltpu.VMEM((HEAD_DIM, blk), F32)],
        compiler_params=_params(("parallel", "parallel"), 48),
        name="fox_attention",
    )(qt, ka, vt)


def _layer_norm(x, g, b):
    mu = jnp.mean(x, axis=-1, keepdims=True)
    xc = x - mu
    var = jnp.mean(xc * xc, axis=-1, keepdims=True)
    return xc * lax.rsqrt(var + EPS) * g + b


def _conv_kernel(a_ref, g_ref, ah_ref, gh_ref, cw_ref, cb_ref, lg_ref, lb_ref, o_ref, u_scr, y_scr, *, bm, kw):
    i = pl.program_id(0)
    n_tiles = cw_ref.shape[0]
    u = a_ref[...].astype(F32) * jax.nn.sigmoid(g_ref[...].astype(F32))
    uh = ah_ref[...].astype(F32) * jax.nn.sigmoid(gh_ref[...].astype(F32))
    uh = jnp.where(i > 0, uh, 0.0)
    n_shifted = bm + CONV_HALO - SUBLANES
    for t in range(n_tiles):
        cs = slice(t * LANES, (t + 1) * LANES)
        u_scr[0, t, :CONV_HALO, :] = uh[:, cs]
        u_scr[0, t, CONV_HALO:, :] = u[:, cs]
        for r in range(1, SUBLANES):
            u_scr[r, t, :n_shifted, :] = u_scr[0, t, pl.ds(r, n_shifted), :]

    def tile(t, carry):
        for r0 in range(0, bm, CONV_STRIP):
            y = jnp.zeros((CONV_STRIP, LANES), F32) + cb_ref[t]
            for j in range(kw):
                q, r = divmod(CONV_HALO - (kw - 1) + j, SUBLANES)
                y = y + cw_ref[t, j:j + 1, :] * u_scr[r, t, r0 + SUBLANES * q:r0 + SUBLANES * q + CONV_STRIP, :]
            y_scr[t, r0:r0 + CONV_STRIP, :] = y
        return carry

    lax.fori_loop(0, n_tiles, tile, 0)
    y = jnp.concatenate([y_scr[t] for t in range(n_tiles)], axis=1)
    y = _layer_norm(y, lg_ref[...], lb_ref[...])
    o_ref[...] = (y * jax.nn.sigmoid(y)).astype(BF16)


def _conformer_conv(proj, a_blk, conv_w, conv_b, ln_g, ln_b, layer):
    s = proj.shape[0]
    _, n_tiles, kw, _ = conv_w.shape
    c = n_tiles * LANES
    assert kw - 1 <= CONV_HALO
    bm = _blk(s, 256)
    assert bm % CONV_STRIP == 0
    per = bm // CONV_HALO
    halo_row = lambda i: jnp.maximum(i * per - 1, 0)
    vec = lambda: pl.BlockSpec((None, 1, c), lambda i: (layer, 0, 0))
    return pl.pallas_call(
        functools.partial(_conv_kernel, bm=bm, kw=kw),
        grid=(s // bm,),
        in_specs=[pl.BlockSpec((bm, c), lambda i: (i, a_blk)),
                  pl.BlockSpec((bm, c), lambda i: (i, a_blk + 1)),
                  pl.BlockSpec((CONV_HALO, c), lambda i: (halo_row(i), a_blk)),
                  pl.BlockSpec((CONV_HALO, c), lambda i: (halo_row(i), a_blk + 1)),
                  pl.BlockSpec((None, n_tiles, kw, LANES), lambda i: (layer, 0, 0, 0)),
                  pl.BlockSpec((None, n_tiles, 1, LANES), lambda i: (layer, 0, 0, 0)), vec(), vec()],
        out_specs=pl.BlockSpec((bm, c), lambda i: (i, 0)),
        out_shape=jax.ShapeDtypeStruct((s, c), BF16),
        scratch_shapes=[pltpu.VMEM((SUBLANES, n_tiles, bm + CONV_HALO, LANES), F32),
                        pltpu.VMEM((n_tiles, bm, LANES), F32)],
        compiler_params=_params(("parallel",), 32),
        name="conformer_conv",
    )(proj, proj, proj, proj, conv_w, conv_b, ln_g, ln_b)


def _gelu_tanh(x):
    return x * (0.5 * (1.0 + jnp.tanh(GELU_C0 * (x + GELU_C1 * (x * x * x)))))


def _sgu_kernel(u_ref, v_ref, w_ref, bt_ref, lg_ref, lb_ref, o_ref, *, bm, chunk, gw):
    u = _gelu_tanh(u_ref[...].astype(F32))
    v = _layer_norm(_gelu_tanh(v_ref[...].astype(F32)), lg_ref[...], lb_ref[...]).astype(BF16)
    row = lax.broadcasted_iota(jnp.int32, (chunk, chunk), 0)
    col = lax.broadcasted_iota(jnp.int32, (chunk, chunk), 1)
    for g in range(w_ref.shape[0]):
        wg = jnp.where(col <= row, w_ref[g], 0.0).astype(BF16)
        bias = bt_ref[:, g:g + 1]
        cs = slice(g * gw, (g + 1) * gw)
        for n in range(bm // chunk):
            rs = slice(n * chunk, (n + 1) * chunk)
            mixed = _dot(wg, v[rs, cs]) + bias
            o_ref[rs, cs] = (u[rs, cs] * mixed).astype(BF16)


def _spatial_gating(proj, u_blk, w_s, b_st, ln_g, ln_b, layer):
    s = proj.shape[0]
    _, ng, chunk, _ = w_s.shape
    c = ln_g.shape[-1]
    gw = c // ng
    bm = _blk(s, 4 * chunk)
    vec = lambda: pl.BlockSpec((None, 1, c), lambda i: (layer, 0, 0))
    return pl.pallas_call(
        functools.partial(_sgu_kernel, bm=bm, chunk=chunk, gw=gw),
        grid=(s // bm,),
        in_specs=[pl.BlockSpec((bm, c), lambda i: (i, u_blk)),
                  pl.BlockSpec((bm, c), lambda i: (i, u_blk + 1)),
                  pl.BlockSpec((None, ng, chunk, chunk), lambda i: (layer, 0, 0, 0)),
                  pl.BlockSpec((None, chunk, ng), lambda i: (layer, 0, 0)), vec(), vec()],
        out_specs=pl.BlockSpec((bm, c), lambda i: (i, 0)),
        out_shape=jax.ShapeDtypeStruct((s, c), BF16),
        compiler_params=_params(("parallel",), 32),
        name="spatial_gating",
    )(proj, proj, w_s, b_st, ln_g, ln_b)


def _merge_kernel(b0_ref, b1_ref, b2_ref, b3_ref, z_ref, wb_ref, wg_ref, bg_ref, o_ref):
    z = z_ref[...]
    merged = None
    for b, br in enumerate((b0_ref, b1_ref, b2_ref, b3_ref)):
        gate = jax.nn.sigmoid(_dot(z, wg_ref[b]) + bg_ref[b:b + 1, :])
        term = gate * _dot(br[...], wb_ref[b])
        merged = term if merged is None else merged + term
    o_ref[...] = merged.astype(BF16)


def _merge(branches, z, w_branch, w_gate_up, b_gate, layer):
    s, c = branches[0].shape
    _, nb, rank, d = w_gate_up.shape
    assert nb == 4 and len(branches) == 4
    bm, bn = _blk(s, 512), _blk(d, 1024)
    act = lambda w: pl.BlockSpec((bm, w), lambda j, i: (i, 0))
    return pl.pallas_call(
        _merge_kernel,
        grid=(d // bn, s // bm),
        in_specs=[act(c), act(c), act(c), act(c), act(rank),
                  pl.BlockSpec((None, nb, c, bn), lambda j, i: (layer, 0, 0, j)),
                  pl.BlockSpec((None, nb, rank, bn), lambda j, i: (layer, 0, 0, j)),
                  pl.BlockSpec((None, nb, bn), lambda j, i: (layer, 0, j))],
        out_specs=pl.BlockSpec((bm, bn), lambda j, i: (i, j)),
        out_shape=jax.ShapeDtypeStruct((s, d), BF16),
        compiler_params=_params(("parallel", "parallel"), 56),
        name="branch_merge",
    )(*branches, z, w_branch, w_gate_up, b_gate)


def _emit_resid(hn, j, g_ref, ho_ref, hg_ref, ssq_ref):
    ho_ref[...] = hn
    hg_ref[...] = (hn * g_ref[...]).astype(BF16)
    part = jnp.sum(hn * hn, axis=-1, keepdims=True)

    @pl.when(j == 0)
    def _():
        ssq_ref[...] = part

    @pl.when(j > 0)
    def _():
        ssq_ref[...] += part


def _resid_kernel(x_ref, w_ref, h_ref, g_ref, ho_ref, hg_ref, ssq_ref):
    hn = h_ref[...] + _dot(x_ref[...], w_ref[...])
    _emit_resid(hn, pl.program_id(1), g_ref, ho_ref, hg_ref, ssq_ref)


def _resid_matmul(x, w, layer, h, g, g_layer, bm_pref, bn_pref):
    s, kdim = x.shape
    d = w.shape[2]
    bm, bn = _blk(s, bm_pref), _blk(d, bn_pref)
    return pl.pallas_call(
        _resid_kernel,
        grid=(s // bm, d // bn),
        in_specs=[pl.BlockSpec((bm, kdim), lambda i, j: (i, 0)),
                  pl.BlockSpec((None, kdim, bn), lambda i, j: (layer, 0, j)),
                  pl.BlockSpec((bm, bn), lambda i, j: (i, j)),
                  pl.BlockSpec((None, 1, bn), lambda i, j: (g_layer, 0, j))],
        out_specs=[pl.BlockSpec((bm, bn), lambda i, j: (i, j)),
                   pl.BlockSpec((bm, bn), lambda i, j: (i, j)),
                   pl.BlockSpec((bm, 1), lambda i, j: (i, 0))],
        out_shape=[jax.ShapeDtypeStruct((s, d), F32), jax.ShapeDtypeStruct((s, d), BF16),
                   jax.ShapeDtypeStruct((s, 1), F32)],
        compiler_params=_params(("parallel", "arbitrary"), 56),
        name="resid_matmul",
    )(x, w, h, g)


def _ffn_act_kernel(x_ref, ssq_ref, wg_ref, wu_ref, o_ref, *, d):
    r = _row_scale(ssq_ref[...], d)
    x = x_ref[...]
    a = _dot(x, wg_ref[...]) * r
    u = _dot(x, wu_ref[...]) * r
    o_ref[...] = (a * jax.nn.sigmoid(a) * u).astype(BF16)


def _ffn_act(hg, ssq, w_gate, w_up, layer):
    s, d = hg.shape
    f = w_gate.shape[2]
    bm, bn = _blk(s, 1024), _blk(f, 512)
    wspec = lambda: pl.BlockSpec((None, d, bn), lambda i, j: (layer, 0, j))
    return pl.pallas_call(
        functools.partial(_ffn_act_kernel, d=d),
        grid=(s // bm, f // bn),
        in_specs=[pl.BlockSpec((bm, d), lambda i, j: (i, 0)), pl.BlockSpec((bm, 1), lambda i, j: (i, 0)),
                  wspec(), wspec()],
        out_specs=pl.BlockSpec((bm, bn), lambda i, j: (i, j)),
        out_shape=jax.ShapeDtypeStruct((s, f), BF16),
        compiler_params=_params(("parallel", "parallel"), 56),
        name="ffn_act",
    )(hg, ssq, w_gate, w_up)


def _ple_kernel(hg_ref, ssq_ref, wd_ref, wu_ref, p_ref, wp_ref, h_ref, g_ref,
                ho_ref, xg_ref, ssqo_ref, t_ref, *, d):
    j = pl.program_id(1)

    @pl.when(j == 0)
    def _():
        t = _dot(hg_ref[...], wd_ref[...]) * _row_scale(ssq_ref[...], d)
        t_ref[...] = t.astype(BF16)

    gate = jax.nn.sigmoid(_dot(t_ref[...], wu_ref[...]))
    inj = _dot(p_ref[...].astype(BF16), wp_ref[...])
    _emit_resid(h_ref[...] + gate * inj, j, g_ref, ho_ref, xg_ref, ssqo_ref)


def _ple(hg, ssq, w_down, w_up, p, w_ple, layer, h, g_next):
    s, d = hg.shape
    rank = w_down.shape[2]
    pdim = p.shape[2]
    bm, bn = _blk(s, 1024), _blk(d, 1024)
    return pl.pallas_call(
        functools.partial(_ple_kernel, d=d),
        grid=(s // bm, d // bn),
        in_specs=[pl.BlockSpec((bm, d), lambda i, j: (i, 0)), pl.BlockSpec((bm, 1), lambda i, j: (i, 0)),
                  pl.BlockSpec((None, d, rank), lambda i, j: (layer, 0, 0)),
                  pl.BlockSpec((None, rank, bn), lambda i, j: (layer, 0, j)),
                  pl.BlockSpec((None, bm, pdim), lambda i, j: (layer, i, 0)),
                  pl.BlockSpec((None, pdim, bn), lambda i, j: (layer, 0, j)),
                  pl.BlockSpec((bm, bn), lambda i, j: (i, j)),
                  pl.BlockSpec((1, bn), lambda i, j: (0, j))],
        out_specs=[pl.BlockSpec((bm, bn), lambda i, j: (i, j)),
                   pl.BlockSpec((bm, bn), lambda i, j: (i, j)),
                   pl.BlockSpec((bm, 1), lambda i, j: (i, 0))],
        out_shape=[jax.ShapeDtypeStruct((s, d), F32), jax.ShapeDtypeStruct((s, d), BF16),
                   jax.ShapeDtypeStruct((s, 1), F32)],
        scratch_shapes=[pltpu.VMEM((bm, rank), BF16)],
        compiler_params=_params(("parallel", "arbitrary"), 56),
        name="per_layer_input",
    )(hg, ssq, w_down, w_up, p, w_ple, h, g_next.reshape(1, d))


def kernel(x, p, norm_mix, w_in, b_f, pool_w, pool_scale, conv_w, conv_b, conv_ln_g, conv_ln_b, sgu_ln_g, sgu_ln_b, sgu_w, sgu_b, w_branch, w_gate_down, w_gate_up, b_gate, w_out, norm_ffn, w_ffn_gate, w_ffn_up, w_ffn_down, norm_ple, w_ple_gate_down, w_ple_gate_up, w_ple, norm_final):
    batch, s, d = x.shape
    assert batch == 1
    depth = w_in.shape[0]
    c = pool_scale.shape[1]
    n_heads = b_f.shape[1]
    rank = w_gate_down.shape[2]
    assert c == n_heads * HEAD_DIM and w_in.shape[2] == 8 * c + n_heads
    off_f = 4 * c

    bf = lambda a: a.astype(BF16)
    w_qkv = bf(w_in[:, :, :off_f])
    w_cs = bf(w_in[:, :, off_f + n_heads:])
    f_cols = jnp.pad(w_in[:, :, off_f:off_f + n_heads], ((0, 0), (0, 0), (0, LANES - n_heads)))
    w_tail = bf(jnp.concatenate([w_gate_down, f_cols], axis=2))
    vec3 = lambda a: a.reshape(depth, 1, a.shape[-1])
    bf_pad = vec3(jnp.pad(b_f, ((0, 0), (0, LANES - n_heads))))
    pool_scale, conv_ln_g, conv_ln_b = vec3(pool_scale), vec3(conv_ln_g), vec3(conv_ln_b)
    kw = conv_w.shape[1]
    conv_w = conv_w.reshape(depth, kw, c // LANES, LANES).transpose(0, 2, 1, 3)
    conv_b = conv_b.reshape(depth, c // LANES, 1, LANES)
    sgu_ln_g, sgu_ln_b, norm_ffn, norm_ple = vec3(sgu_ln_g), vec3(sgu_ln_b), vec3(norm_ffn), vec3(norm_ple)
    ones4c = jnp.ones((1, 4 * c), F32)
    qkv_scale = ones4c.at[:, c:2 * c].set(HEAD_DIM ** -0.5 * LOG2E)
    pool_wb, w_branch_b, w_gate_up_b, w_out_b = bf(pool_w), bf(w_branch), bf(w_gate_up), bf(w_out)
    w_fg, w_fu, w_fd = bf(w_ffn_gate), bf(w_ffn_up), bf(w_ffn_down)
    w_pd, w_pu, w_pl = bf(w_ple_gate_down), bf(w_ple_gate_up), bf(w_ple)
    sgu_bt = sgu_b.transpose(0, 2, 1)
    p3 = p.reshape(depth, s, p.shape[-1])

    h = x.reshape(s, d)
    hg, ssq = _norm_prep(h, norm_mix[0])
    for i in range(depth):
        qkv = _in_proj(hg, ssq, w_qkv, i, qkv_scale)
        cs = _in_proj(hg, ssq, w_cs, i, ones4c)
        z, f_pieces = _gate_tail(hg, ssq, w_tail, i, bf_pad, rank)
        branches = (
            _pool_mixer(qkv, 0, pool_wb, i, pool_scale),
            _fox_attention(qkv, f_pieces, n_heads),
            _conformer_conv(cs, 0, conv_w, conv_b, conv_ln_g, conv_ln_b, i),
            _spatial_gating(cs, 2, sgu_w, sgu_bt, sgu_ln_g, sgu_ln_b, i),
        )
        merged = _merge(branches, z, w_branch_b, w_gate_up_b, b_gate, i)
        h, hg, ssq = _resid_matmul(merged, w_out_b, i, h, norm_ffn, i, 1024, 512)

        act = _ffn_act(hg, ssq, w_fg, w_fu, i)
        h, hg, ssq = _resid_matmul(act, w_fd, i, h, norm_ple, i, 512, 512)

        g_next = norm_mix[i + 1] if i + 1 < depth else norm_final
        h, hg, ssq = _ple(hg, ssq, w_pd, w_pu, p3, w_pl, i, h, g_next)
    return _final_norm(h, ssq, norm_final).reshape(batch, s, d)
```

```python
import functools
import math

import jax
import jax.numpy as jnp
from jax import lax
from jax.experimental import pallas as pl
from jax.experimental.pallas import tpu as pltpu

F32 = jnp.float32
BF16 = jnp.bfloat16

EPS = 1e-6
HEAD_DIM = 128
POOL_WINDOWS = (2, 4, 8, 16)
GELU_C0 = math.sqrt(2.0 / math.pi)
GELU_C1 = 0.044715
LOG2E = math.log2(math.e)

LANES = 128
MXU_DEPTH = 256
MIB = 1024 * 1024

POOL_HALO = 16
CONV_HALO = 32
N_BIAS_PIECES = 3
ATTN_GROUP = 4
EXP2_UNDERFLOW = 160.0
CONV_STRIP = 128
SUBLANES = 8


def _params(semantics, vmem_mib):
    return pltpu.CompilerParams(dimension_semantics=semantics, vmem_limit_bytes=vmem_mib * MIB)


def _blk(n, pref):
    b = min(n, pref)
    while n % b:
        b -= LANES
    assert b > 0
    return b


def _row_scale(ssq, d):
    return lax.rsqrt(ssq / d + EPS)


def _dot(a, b):
    return jnp.dot(a, b, preferred_element_type=F32)


def _split3(x):
    hi = x.astype(BF16)
    rem = x - hi.astype(F32)
    mid = rem.astype(BF16)
    lo = (rem - mid.astype(F32)).astype(BF16)
    return hi, mid, lo


def _prep_kernel(x_ref, g_ref, hg_ref, ssq_ref):
    x = x_ref[...]
    ssq_ref[...] = jnp.sum(x * x, axis=-1, keepdims=True)
    hg_ref[...] = (x * g_ref[...]).astype(BF16)


def _norm_prep(x, g):
    s, d = x.shape
    bm = _blk(s, 256)
    return pl.pallas_call(
        _prep_kernel,
        grid=(s // bm,),
        in_specs=[pl.BlockSpec((bm, d), lambda i: (i, 0)), pl.BlockSpec((1, d), lambda i: (0, 0))],
        out_specs=[pl.BlockSpec((bm, d), lambda i: (i, 0)), pl.BlockSpec((bm, 1), lambda i: (i, 0))],
        out_shape=[jax.ShapeDtypeStruct((s, d), BF16), jax.ShapeDtypeStruct((s, 1), F32)],
        compiler_params=_params(("parallel",), 32),
        name="norm_prep",
    )(x, g.reshape(1, d))


def _final_kernel(h_ref, ssq_ref, g_ref, o_ref, *, d):
    y = h_ref[...] * _row_scale(ssq_ref[...], d)
    o_ref[...] = y * g_ref[...]


def _final_norm(h, ssq, g):
    s, d = h.shape
    bm = _blk(s, 256)
    return pl.pallas_call(
        functools.partial(_final_kernel, d=d),
        grid=(s // bm,),
        in_specs=[pl.BlockSpec((bm, d), lambda i: (i, 0)), pl.BlockSpec((bm, 1), lambda i: (i, 0)),
                  pl.BlockSpec((1, d), lambda i: (0, 0))],
        out_specs=pl.BlockSpec((bm, d), lambda i: (i, 0)),
        out_shape=jax.ShapeDtypeStruct((s, d), F32),
        compiler_params=_params(("parallel",), 32),
        name="final_norm",
    )(h, ssq, g.reshape(1, d))


def _in_kernel(x_ref, ssq_ref, w_ref, cs_ref, o_ref, *, d):
    acc = _dot(x_ref[...], w_ref[...].astype(BF16))
    o_ref[...] = (acc * _row_scale(ssq_ref[...], d) * cs_ref[...]).astype(o_ref.dtype)


def _in_proj(hg, ssq, w, layer, col_scale):
    s, d = hg.shape
    n = col_scale.shape[1]
    bm, bn = _blk(s, 1024), _blk(n, 1024 if w.dtype == BF16 else 512)
    return pl.pallas_call(
        functools.partial(_in_kernel, d=d),
        grid=(s // bm, n // bn),
        in_specs=[pl.BlockSpec((bm, d), lambda i, j: (i, 0)), pl.BlockSpec((bm, 1), lambda i, j: (i, 0)),
                  pl.BlockSpec((None, d, bn), lambda i, j: (layer, 0, j)),
                  pl.BlockSpec((1, bn), lambda i, j: (0, j))],
        out_specs=pl.BlockSpec((bm, bn), lambda i, j: (i, j)),
        out_shape=jax.ShapeDtypeStruct((s, n), BF16),
        compiler_params=_params(("parallel", "parallel"), 56),
        name="in_proj",
    )(hg, ssq, w, col_scale)


def _tail_kernel(x_ref, ssq_ref, w_ref, bf_ref, z_ref, f_ref, carry_ref, *, d, rank, bm):
    i = pl.program_id(0)

    @pl.when(i == 0)
    def _():
        carry_ref[...] = jnp.zeros_like(carry_ref)

    acc = _dot(x_ref[...], w_ref[...]) * _row_scale(ssq_ref[...], d)
    z_ref[...] = acc[:, :rank].astype(BF16)
    logit = acc[:, rank:] + bf_ref[...]
    logf = jnp.minimum(logit, 0.0) - jnp.log1p(jnp.exp(-jnp.abs(logit)))
    row = lax.broadcasted_iota(jnp.int32, (bm, bm), 0)
    col = lax.broadcasted_iota(jnp.int32, (bm, bm), 1)
    tri = (col <= row).astype(BF16)
    hi, mid, lo = _split3(logf)
    cs = _dot(tri, hi) + _dot(tri, mid) + _dot(tri, lo) + carry_ref[...]
    carry_ref[...] = cs[bm - 1:bm, :]
    for n, piece in enumerate(_split3(cs * LOG2E)):
        f_ref[:, n * LANES:(n + 1) * LANES] = piece


def _gate_tail(hg, ssq, w_tail, layer, bf_pad, rank):
    s, d = hg.shape
    n = w_tail.shape[2]
    bm = _blk(s, 512)
    return pl.pallas_call(
        functools.partial(_tail_kernel, d=d, rank=rank, bm=bm),
        grid=(s // bm,),
        in_specs=[pl.BlockSpec((bm, d), lambda i: (i, 0)), pl.BlockSpec((bm, 1), lambda i: (i, 0)),
                  pl.BlockSpec((None, d, n), lambda i: (layer, 0, 0)),
                  pl.BlockSpec((None, 1, LANES), lambda i: (layer, 0, 0))],
        out_specs=[pl.BlockSpec((bm, rank), lambda i: (i, 0)),
                   pl.BlockSpec((bm, N_BIAS_PIECES * LANES), lambda i: (i, 0))],
        out_shape=[jax.ShapeDtypeStruct((s, rank), BF16),
                   jax.ShapeDtypeStruct((s, N_BIAS_PIECES * LANES), BF16)],
        scratch_shapes=[pltpu.VMEM((1, LANES), F32)],
        compiler_params=_params(("arbitrary",), 40),
        name="gate_tail",
    )(hg, ssq, w_tail, bf_pad)


def _pool_kernel(x_ref, halo_ref, w_ref, sc_ref, o_ref, *, bm, gw):
    i = pl.program_id(0)
    x = x_ref[...].astype(F32)
    halo = jnp.where(i > 0, halo_ref[...].astype(F32), 0.0)
    xx = jnp.concatenate([halo, x], axis=0)
    pos = i * bm + lax.broadcasted_iota(jnp.int32, (bm, 1), 0)
    for g, w in enumerate(POOL_WINDOWS):
        sl = slice(g * gw, (g + 1) * gw)
        s = xx[:, sl]
        span = 1
        while span < w:
            s = s + pltpu.roll(s, span, axis=0)
            span *= 2
        cnt = jnp.minimum(pos + 1, w).astype(F32)
        pooled = s[POOL_HALO:, :] / cnt - x[:, sl]
        mixed = _dot(pooled.astype(BF16), w_ref[g])
        o_ref[:, sl] = (mixed * sc_ref[:, sl]).astype(BF16)


def _pool_mixer(proj, col_blk, pool_w, layer, pool_scale):
    s = proj.shape[0]
    _, ng, gw, _ = pool_w.shape
    c = ng * gw
    assert all(w & (w - 1) == 0 and w - 1 <= POOL_HALO for w in POOL_WINDOWS) and ng == len(POOL_WINDOWS)
    bm = _blk(s, 512)
    per = bm // POOL_HALO
    return pl.pallas_call(
        functools.partial(_pool_kernel, bm=bm, gw=gw),
        grid=(s // bm,),
        in_specs=[pl.BlockSpec((bm, c), lambda i: (i, col_blk)),
                  pl.BlockSpec((POOL_HALO, c), lambda i: (jnp.maximum(i * per - 1, 0), col_blk)),
                  pl.BlockSpec((None, ng, gw, gw), lambda i: (layer, 0, 0, 0)),
                  pl.BlockSpec((None, 1, c), lambda i: (layer, 0, 0))],
        out_specs=pl.BlockSpec((bm, c), lambda i: (i, 0)),
        out_shape=jax.ShapeDtypeStruct((s, c), BF16),
        compiler_params=_params(("parallel",), 32),
        name="pool_mixer",
    )(proj, proj, pool_w, pool_scale)


def _attn_kernel(first_ref, qt_ref, k_ref, vt_ref, o_ref, st0, st1, p0, p1, acc_ref, *, blk):
    i = pl.program_id(1)
    first = first_ref[pl.program_id(0) * pl.num_programs(1) + i]
    qt = qt_ref[...]

    def scores(j):
        off = pl.multiple_of(j * blk, blk)
        return _dot(k_ref[pl.ds(off, blk), :], qt)

    def softmax(st_ref, p_ref, m, l, masked):
        st = st_ref[...]
        if masked:
            key = lax.broadcasted_iota(jnp.int32, (blk, blk), 0)
            qry = lax.broadcasted_iota(jnp.int32, (blk, blk), 1)
            st = jnp.where(key <= qry, st, -jnp.inf)
        m_new = jnp.maximum(m, jnp.max(st, axis=0, keepdims=True))
        alpha = jnp.exp2(m - m_new)
        p = jnp.exp2(st - m_new)
        p_ref[...] = p.astype(BF16)
        return m_new, alpha * l + jnp.sum(p, axis=0, keepdims=True), alpha

    def pv(p_ref, j):
        return _dot(vt_ref[j], p_ref[...])

    st_bufs, p_bufs = (st0, st1), (p0, p1)

    def group(a, m, l, n, tail):
        for b in range(n):
            last = tail and b == n - 1
            if not last:
                st_bufs[(b + 1) % 2][...] = scores(a + b + 1)
            owed = pv(p_bufs[(b + 1) % 2], jnp.maximum(a + b - 1, 0))
            m, l, alpha = softmax(st_bufs[b % 2], p_bufs[b % 2], m, l, last)
            acc_ref[...] = (acc_ref[...] + owed) * alpha
        return m, l

    p1[...] = jnp.zeros_like(p1)
    acc_ref[...] = jnp.zeros_like(acc_ref)
    st0[...] = scores(first)
    m0 = jnp.full((1, blk), -jnp.inf, F32)
    l0 = jnp.zeros((1, blk), F32)
    trips = (i - first) // ATTN_GROUP
    m, l = lax.fori_loop(0, trips, lambda t, c: group(first + ATTN_GROUP * t, c[0], c[1], ATTN_GROUP, False),
                         (m0, l0))
    rem = i - first + 1 - ATTN_GROUP * trips

    for n in range(1, ATTN_GROUP + 1):
        @pl.when(rem == n)
        def _(n=n):
            _, l_fin = group(i + 1 - n, m, l, n, True)
            out = (acc_ref[...] + pv(p_bufs[(n - 1) % 2], i)) / l_fin
            o_ref[...] = out.T.astype(o_ref.dtype)


def _fox_attention(proj, f_pieces, n_heads):
    s = proj.shape[0]
    c = n_heads * HEAD_DIM
    blk = _blk(s, 512)
    nb = s // blk
    heads = lambda a: a.reshape(s, n_heads, HEAD_DIM)
    pieces = f_pieces.reshape(s, N_BIAS_PIECES, LANES)[:, :, :n_heads].transpose(0, 2, 1)
    ones = jnp.ones_like(pieces)
    zpad = jnp.zeros((s, n_heads, MXU_DEPTH - HEAD_DIM - 2 * N_BIAS_PIECES), BF16)
    q_aug = jnp.concatenate([heads(proj[:, c:2 * c]), pieces, ones, zpad], axis=-1)
    k_aug = jnp.concatenate([heads(proj[:, 2 * c:3 * c]), ones, -pieces, zpad], axis=-1)
    qt = q_aug.transpose(1, 2, 0)
    ka = k_aug.transpose(1, 0, 2)
    vt = proj[:, 3 * c:4 * c].reshape(nb, blk, n_heads, HEAD_DIM).transpose(2, 0, 3, 1)
    first = _first_contributing_block(q_aug, k_aug, pieces, nb, blk)
    grid_spec = pltpu.PrefetchScalarGridSpec(
        num_scalar_prefetch=1,
        grid=(n_heads, nb),
        in_specs=[pl.BlockSpec((None, MXU_DEPTH, blk), lambda h, i, first: (h, 0, i)),
                  pl.BlockSpec((None, s, MXU_DEPTH), lambda h, i, first: (h, 0, 0)),
                  pl.BlockSpec((None, nb, HEAD_DIM, blk), lambda h, i, first: (h, 0, 0, 0))],
        out_specs=pl.BlockSpec((blk, HEAD_DIM), lambda h, i, first: (i, h)),
        scratch_shapes=[pltpu.VMEM((blk, blk), F32), pltpu.VMEM((blk, blk), F32),
                        pltpu.VMEM((blk, blk), BF16), pltpu.VMEM((blk, blk), BF16),
                        pltpu.VMEM((HEAD_DIM, blk), F32)])
    return pl.pallas_call(
        functools.partial(_attn_kernel, blk=blk),
        grid_spec=grid_spec,
        out_shape=jax.ShapeDtypeStruct((s, c), BF16),
        compiler_params=_params(("parallel", "parallel"), 48),
        name="fox_attention",
    )(first, qt, ka, vt)


def _first_contributing_block(q_aug, k_aug, pieces, nb, blk):
    s, n_heads, _ = q_aug.shape
    norm = lambda a: jnp.sqrt(jnp.max(jnp.sum(jnp.square(a[:, :, :HEAD_DIM].astype(F32)), axis=-1), axis=0))
    slack = 2.0 * norm(q_aug) * norm(k_aug) * (1.0 + 2.0 ** -6) + 2.0
    f = jnp.sum(pieces.astype(F32), axis=-1).reshape(nb, blk, n_heads)
    f_max, f_min = jnp.max(f, axis=1), jnp.min(f, axis=1)
    bound = f_max[:, None, :] - f_min[None, :, :] + slack
    needed = bound >= -EXP2_UNDERFLOW
    needed = needed | (jnp.arange(nb)[:, None] == jnp.arange(nb)[None, :])[:, :, None]
    first = jnp.argmax(needed, axis=1).astype(jnp.int32)
    return first.T.reshape(n_heads * nb)


def _layer_norm(x, g, b):
    mu = jnp.mean(x, axis=-1, keepdims=True)
    xc = x - mu
    var = jnp.mean(xc * xc, axis=-1, keepdims=True)
    return xc * lax.rsqrt(var + EPS) * g + b


def _conv_kernel(a_ref, g_ref, ah_ref, gh_ref, cw_ref, cb_ref, lg_ref, lb_ref, o_ref, u_scr, y_scr, *, bm, kw):
    i = pl.program_id(0)
    n_tiles = cw_ref.shape[0]
    u = a_ref[...].astype(F32) * jax.nn.sigmoid(g_ref[...].astype(F32))
    uh = ah_ref[...].astype(F32) * jax.nn.sigmoid(gh_ref[...].astype(F32))
    uh = jnp.where(i > 0, uh, 0.0)
    n_shifted = bm + CONV_HALO - SUBLANES
    for t in range(n_tiles):
        cs = slice(t * LANES, (t + 1) * LANES)
        u_scr[0, t, :CONV_HALO, :] = uh[:, cs]
        u_scr[0, t, CONV_HALO:, :] = u[:, cs]
        for r in range(1, SUBLANES):
            u_scr[r, t, :n_shifted, :] = u_scr[0, t, pl.ds(r, n_shifted), :]

    def tile(t, carry):
        for r0 in range(0, bm, CONV_STRIP):
            y = jnp.zeros((CONV_STRIP, LANES), F32) + cb_ref[t]
            for j in range(kw):
                q, r = divmod(CONV_HALO - (kw - 1) + j, SUBLANES)
                y = y + cw_ref[t, j:j + 1, :] * u_scr[r, t, r0 + SUBLANES * q:r0 + SUBLANES * q + CONV_STRIP, :]
            y_scr[t, r0:r0 + CONV_STRIP, :] = y
        return carry

    lax.fori_loop(0, n_tiles, tile, 0)
    y = jnp.concatenate([y_scr[t] for t in range(n_tiles)], axis=1)
    y = _layer_norm(y, lg_ref[...], lb_ref[...])
    o_ref[...] = (y * jax.nn.sigmoid(y)).astype(BF16)


def _conformer_conv(proj, a_blk, conv_w, conv_b, ln_g, ln_b, layer):
    s = proj.shape[0]
    _, n_tiles, kw, _ = conv_w.shape
    c = n_tiles * LANES
    assert kw - 1 <= CONV_HALO
    bm = _blk(s, 256)
    assert bm % CONV_STRIP == 0
    per = bm // CONV_HALO
    halo_row = lambda i: jnp.maximum(i * per - 1, 0)
    vec = lambda: pl.BlockSpec((None, 1, c), lambda i: (layer, 0, 0))
    return pl.pallas_call(
        functools.partial(_conv_kernel, bm=bm, kw=kw),
        grid=(s // bm,),
        in_specs=[pl.BlockSpec((bm, c), lambda i: (i, a_blk)),
                  pl.BlockSpec((bm, c), lambda i: (i, a_blk + 1)),
                  pl.BlockSpec((CONV_HALO, c), lambda i: (halo_row(i), a_blk)),
                  pl.BlockSpec((CONV_HALO, c), lambda i: (halo_row(i), a_blk + 1)),
                  pl.BlockSpec((None, n_tiles, kw, LANES), lambda i: (layer, 0, 0, 0)),
                  pl.BlockSpec((None, n_tiles, 1, LANES), lambda i: (layer, 0, 0, 0)), vec(), vec()],
        out_specs=pl.BlockSpec((bm, c), lambda i: (i, 0)),
        out_shape=jax.ShapeDtypeStruct((s, c), BF16),
        scratch_shapes=[pltpu.VMEM((SUBLANES, n_tiles, bm + CONV_HALO, LANES), F32),
                        pltpu.VMEM((n_tiles, bm, LANES), F32)],
        compiler_params=_params(("parallel",), 32),
        name="conformer_conv",
    )(proj, proj, proj, proj, conv_w, conv_b, ln_g, ln_b)


def _gelu_tanh(x):
    return x * (0.5 * (1.0 + jnp.tanh(GELU_C0 * (x + GELU_C1 * (x * x * x)))))


def _sgu_kernel(u_ref, v_ref, w_ref, bt_ref, lg_ref, lb_ref, o_ref, *, bm, chunk, gw):
    u = _gelu_tanh(u_ref[...].astype(F32))
    v = _layer_norm(_gelu_tanh(v_ref[...].astype(F32)), lg_ref[...], lb_ref[...]).astype(BF16)
    row = lax.broadcasted_iota(jnp.int32, (chunk, chunk), 0)
    col = lax.broadcasted_iota(jnp.int32, (chunk, chunk), 1)
    for g in range(w_ref.shape[0]):
        wg = jnp.where(col <= row, w_ref[g], 0.0).astype(BF16)
        bias = bt_ref[:, g:g + 1]
        cs = slice(g * gw, (g + 1) * gw)
        for n in range(bm // chunk):
            rs = slice(n * chunk, (n + 1) * chunk)
            mixed = _dot(wg, v[rs, cs]) + bias
            o_ref[rs, cs] = (u[rs, cs] * mixed).astype(BF16)


def _spatial_gating(proj, u_blk, w_s, b_st, ln_g, ln_b, layer):
    s = proj.shape[0]
    _, ng, chunk, _ = w_s.shape
    c = ln_g.shape[-1]
    gw = c // ng
    bm = _blk(s, 4 * chunk)
    vec = lambda: pl.BlockSpec((None, 1, c), lambda i: (layer, 0, 0))
    return pl.pallas_call(
        functools.partial(_sgu_kernel, bm=bm, chunk=chunk, gw=gw),
        grid=(s // bm,),
        in_specs=[pl.BlockSpec((bm, c), lambda i: (i, u_blk)),
                  pl.BlockSpec((bm, c), lambda i: (i, u_blk + 1)),
                  pl.BlockSpec((None, ng, chunk, chunk), lambda i: (layer, 0, 0, 0)),
                  pl.BlockSpec((None, chunk, ng), lambda i: (layer, 0, 0)), vec(), vec()],
        out_specs=pl.BlockSpec((bm, c), lambda i: (i, 0)),
        out_shape=jax.ShapeDtypeStruct((s, c), BF16),
        compiler_params=_params(("parallel",), 32),
        name="spatial_gating",
    )(proj, proj, w_s, b_st, ln_g, ln_b)


def _merge_kernel(b0_ref, b1_ref, b2_ref, b3_ref, z_ref, wb_ref, wg_ref, bg_ref, o_ref):
    z = z_ref[...]
    merged = None
    for b, br in enumerate((b0_ref, b1_ref, b2_ref, b3_ref)):
        gate = jax.nn.sigmoid(_dot(z, wg_ref[b]) + bg_ref[b:b + 1, :])
        term = gate * _dot(br[...], wb_ref[b])
        merged = term if merged is None else merged + term
    o_ref[...] = merged.astype(BF16)


def _merge(branches, z, w_branch, w_gate_up, b_gate, layer):
    s, c = branches[0].shape
    _, nb, rank, d = w_gate_up.shape
    assert nb == 4 and len(branches) == 4
    bm, bn = _blk(s, 512), _blk(d, 1024)
    act = lambda w: pl.BlockSpec((bm, w), lambda j, i: (i, 0))
    return pl.pallas_call(
        _merge_kernel,
        grid=(d // bn, s // bm),
        in_specs=[act(c), act(c), act(c), act(c), act(rank),
                  pl.BlockSpec((None, nb, c, bn), lambda j, i: (layer, 0, 0, j)),
                  pl.BlockSpec((None, nb, rank, bn), lambda j, i: (layer, 0, 0, j)),
                  pl.BlockSpec((None, nb, bn), lambda j, i: (layer, 0, j))],
        out_specs=pl.BlockSpec((bm, bn), lambda j, i: (i, j)),
        out_shape=jax.ShapeDtypeStruct((s, d), BF16),
        compiler_params=_params(("parallel", "parallel"), 56),
        name="branch_merge",
    )(*branches, z, w_branch, w_gate_up, b_gate)


def _emit_resid(hn, j, g_ref, ho_ref, hg_ref, ssq_ref):
    ho_ref[...] = hn
    hg_ref[...] = (hn * g_ref[...]).astype(BF16)
    part = jnp.sum(hn * hn, axis=-1, keepdims=True)

    @pl.when(j == 0)
    def _():
        ssq_ref[...] = part

    @pl.when(j > 0)
    def _():
        ssq_ref[...] += part


def _resid_kernel(x_ref, w_ref, h_ref, g_ref, ho_ref, hg_ref, ssq_ref):
    hn = h_ref[...] + _dot(x_ref[...], w_ref[...])
    _emit_resid(hn, pl.program_id(1), g_ref, ho_ref, hg_ref, ssq_ref)


def _resid_matmul(x, w, layer, h, g, g_layer, bm_pref, bn_pref):
    s, kdim = x.shape
    d = w.shape[2]
    bm, bn = _blk(s, bm_pref), _blk(d, bn_pref)
    return pl.pallas_call(
        _resid_kernel,
        grid=(s // bm, d // bn),
        in_specs=[pl.BlockSpec((bm, kdim), lambda i, j: (i, 0)),
                  pl.BlockSpec((None, kdim, bn), lambda i, j: (layer, 0, j)),
                  pl.BlockSpec((bm, bn), lambda i, j: (i, j)),
                  pl.BlockSpec((None, 1, bn), lambda i, j: (g_layer, 0, j))],
        out_specs=[pl.BlockSpec((bm, bn), lambda i, j: (i, j)),
                   pl.BlockSpec((bm, bn), lambda i, j: (i, j)),
                   pl.BlockSpec((bm, 1), lambda i, j: (i, 0))],
        out_shape=[jax.ShapeDtypeStruct((s, d), F32), jax.ShapeDtypeStruct((s, d), BF16),
                   jax.ShapeDtypeStruct((s, 1), F32)],
        compiler_params=_params(("parallel", "arbitrary"), 56),
        name="resid_matmul",
    )(x, w, h, g)


def _ffn_act_kernel(x_ref, ssq_ref, wg_ref, wu_ref, o_ref, *, d):
    r = _row_scale(ssq_ref[...], d)
    x = x_ref[...]
    a = _dot(x, wg_ref[...]) * r
    u = _dot(x, wu_ref[...]) * r
    o_ref[...] = (a * jax.nn.sigmoid(a) * u).astype(BF16)


def _ffn_act(hg, ssq, w_gate, w_up, layer):
    s, d = hg.shape
    f = w_gate.shape[2]
    bm, bn = _blk(s, 1024), _blk(f, 512)
    wspec = lambda: pl.BlockSpec((None, d, bn), lambda i, j: (layer, 0, j))
    return pl.pallas_call(
        functools.partial(_ffn_act_kernel, d=d),
        grid=(s // bm, f // bn),
        in_specs=[pl.BlockSpec((bm, d), lambda i, j: (i, 0)), pl.BlockSpec((bm, 1), lambda i, j: (i, 0)),
                  wspec(), wspec()],
        out_specs=pl.BlockSpec((bm, bn), lambda i, j: (i, j)),
        out_shape=jax.ShapeDtypeStruct((s, f), BF16),
        compiler_params=_params(("parallel", "parallel"), 56),
        name="ffn_act",
    )(hg, ssq, w_gate, w_up)


def _ple_kernel(hg_ref, ssq_ref, wd_ref, wu_ref, p_ref, wp_ref, h_ref, g_ref,
                ho_ref, xg_ref, ssqo_ref, t_ref, *, d):
    j = pl.program_id(1)

    @pl.when(j == 0)
    def _():
        t = _dot(hg_ref[...], wd_ref[...]) * _row_scale(ssq_ref[...], d)
        t_ref[...] = t.astype(BF16)

    gate = jax.nn.sigmoid(_dot(t_ref[...], wu_ref[...]))
    inj = _dot(p_ref[...].astype(BF16), wp_ref[...])
    _emit_resid(h_ref[...] + gate * inj, j, g_ref, ho_ref, xg_ref, ssqo_ref)


def _ple(hg, ssq, w_down, w_up, p, w_ple, layer, h, g_next):
    s, d = hg.shape
    rank = w_down.shape[2]
    pdim = p.shape[2]
    bm, bn = _blk(s, 1024), _blk(d, 1024)
    return pl.pallas_call(
        functools.partial(_ple_kernel, d=d),
        grid=(s // bm, d // bn),
        in_specs=[pl.BlockSpec((bm, d), lambda i, j: (i, 0)), pl.BlockSpec((bm, 1), lambda i, j: (i, 0)),
                  pl.BlockSpec((None, d, rank), lambda i, j: (layer, 0, 0)),
                  pl.BlockSpec((None, rank, bn), lambda i, j: (layer, 0, j)),
                  pl.BlockSpec((None, bm, pdim), lambda i, j: (layer, i, 0)),
                  pl.BlockSpec((None, pdim, bn), lambda i, j: (layer, 0, j)),
                  pl.BlockSpec((bm, bn), lambda i, j: (i, j)),
                  pl.BlockSpec((1, bn), lambda i, j: (0, j))],
        out_specs=[pl.BlockSpec((bm, bn), lambda i, j: (i, j)),
                   pl.BlockSpec((bm, bn), lambda i, j: (i, j)),
                   pl.BlockSpec((bm, 1), lambda i, j: (i, 0))],
        out_shape=[jax.ShapeDtypeStruct((s, d), F32), jax.ShapeDtypeStruct((s, d), BF16),
                   jax.ShapeDtypeStruct((s, 1), F32)],
        scratch_shapes=[pltpu.VMEM((bm, rank), BF16)],
        compiler_params=_params(("parallel", "arbitrary"), 56),
        name="per_layer_input",
    )(hg, ssq, w_down, w_up, p, w_ple, h, g_next.reshape(1, d))


def kernel(x, p, norm_mix, w_in, b_f, pool_w, pool_scale, conv_w, conv_b, conv_ln_g, conv_ln_b, sgu_ln_g, sgu_ln_b, sgu_w, sgu_b, w_branch, w_gate_down, w_gate_up, b_gate, w_out, norm_ffn, w_ffn_gate, w_ffn_up, w_ffn_down, norm_ple, w_ple_gate_down, w_ple_gate_up, w_ple, norm_final):
    batch, s, d = x.shape
    assert batch == 1
    depth = w_in.shape[0]
    c = pool_scale.shape[1]
    n_heads = b_f.shape[1]
    rank = w_gate_down.shape[2]
    assert c == n_heads * HEAD_DIM and w_in.shape[2] == 8 * c + n_heads
    off_f = 4 * c

    bf = lambda a: a.astype(BF16)
    w_cs = bf(w_in[:, :, off_f + n_heads:])
    f_cols = jnp.pad(w_in[:, :, off_f:off_f + n_heads], ((0, 0), (0, 0), (0, LANES - n_heads)))
    w_tail = bf(jnp.concatenate([w_gate_down, f_cols], axis=2))
    vec3 = lambda a: a.reshape(depth, 1, a.shape[-1])
    bf_pad = vec3(jnp.pad(b_f, ((0, 0), (0, LANES - n_heads))))
    pool_scale, conv_ln_g, conv_ln_b = vec3(pool_scale), vec3(conv_ln_g), vec3(conv_ln_b)
    kw = conv_w.shape[1]
    conv_w = conv_w.reshape(depth, kw, c // LANES, LANES).transpose(0, 2, 1, 3)
    conv_b = conv_b.reshape(depth, c // LANES, 1, LANES)
    sgu_ln_g, sgu_ln_b, norm_ffn, norm_ple = vec3(sgu_ln_g), vec3(sgu_ln_b), vec3(norm_ffn), vec3(norm_ple)
    ones4c = jnp.ones((1, 4 * c), F32)
    qkv_scale = ones4c.at[:, c:2 * c].set(HEAD_DIM ** -0.5 * LOG2E)
    pool_wb, w_branch_b, w_gate_up_b, w_out_b = bf(pool_w), bf(w_branch), bf(w_gate_up), bf(w_out)
    w_fg, w_fu, w_fd = bf(w_ffn_gate), bf(w_ffn_up), bf(w_ffn_down)
    w_pd, w_pu, w_pl = bf(w_ple_gate_down), bf(w_ple_gate_up), bf(w_ple)
    sgu_bt = sgu_b.transpose(0, 2, 1)
    p3 = p.reshape(depth, s, p.shape[-1])

    h = x.reshape(s, d)
    hg, ssq = _norm_prep(h, norm_mix[0])
    for i in range(depth):
        qkv = _in_proj(hg, ssq, w_in, i, qkv_scale)
        cs = _in_proj(hg, ssq, w_cs, i, ones4c)
        z, f_pieces = _gate_tail(hg, ssq, w_tail, i, bf_pad, rank)
        branches = (
            _pool_mixer(qkv, 0, pool_wb, i, pool_scale),
            _fox_attention(qkv, f_pieces, n_heads),
            _conformer_conv(cs, 0, conv_w, conv_b, conv_ln_g, conv_ln_b, i),
            _spatial_gating(cs, 2, sgu_w, sgu_bt, sgu_ln_g, sgu_ln_b, i),
        )
        merged = _merge(branches, z, w_branch_b, w_gate_up_b, b_gate, i)
        h, hg, ssq = _resid_matmul(merged, w_out_b, i, h, norm_ffn, i, 1024, 512)

        act = _ffn_act(hg, ssq, w_fg, w_fu, i)
        h, hg, ssq = _resid_matmul(act, w_fd, i, h, norm_ple, i, 512, 512)

        g_next = norm_mix[i + 1] if i + 1 < depth else norm_final
        h, hg, ssq = _ple(hg, ssq, w_pd, w_pu, p3, w_pl, i, h, g_next)
    return _final_norm(h, ssq, norm_final).reshape(batch, s, d)
```

```python
import functools
import math

import jax
import jax.numpy as jnp
from jax import lax
from jax.experimental import pallas as pl
from jax.experimental.pallas import tpu as pltpu

F32 = jnp.float32
BF16 = jnp.bfloat16

EPS = 1e-6
HEAD_DIM = 128
POOL_WINDOWS = (2, 4, 8, 16)
GELU_C0 = math.sqrt(2.0 / math.pi)
GELU_C1 = 0.044715
LOG2E = math.log2(math.e)

LANES = 128
MXU_DEPTH = 256
MIB = 1024 * 1024

POOL_HALO = 16
CONV_HALO = 32
N_BIAS_PIECES = 3
ATTN_GROUP = 4
EXP2_UNDERFLOW = 160.0
CONV_STRIP = 128
SUBLANES = 8


def _params(semantics, vmem_mib):
    return pltpu.CompilerParams(dimension_semantics=semantics, vmem_limit_bytes=vmem_mib * MIB)


def _blk(n, pref):
    b = min(n, pref)
    while n % b:
        b -= LANES
    assert b > 0
    return b


def _row_scale(ssq, d):
    return lax.rsqrt(ssq / d + EPS)


def _dot(a, b):
    return jnp.dot(a, b, preferred_element_type=F32)


def _split3(x):
    hi = x.astype(BF16)
    rem = x - hi.astype(F32)
    mid = rem.astype(BF16)
    lo = (rem - mid.astype(F32)).astype(BF16)
    return hi, mid, lo


def _prep_kernel(x_ref, g_ref, hg_ref, ssq_ref):
    x = x_ref[...]
    ssq_ref[...] = jnp.sum(x * x, axis=-1, keepdims=True)
    hg_ref[...] = (x * g_ref[...]).astype(BF16)


def _norm_prep(x, g):
    s, d = x.shape
    bm = _blk(s, 256)
    return pl.pallas_call(
        _prep_kernel,
        grid=(s // bm,),
        in_specs=[pl.BlockSpec((bm, d), lambda i: (i, 0)), pl.BlockSpec((1, d), lambda i: (0, 0))],
        out_specs=[pl.BlockSpec((bm, d), lambda i: (i, 0)), pl.BlockSpec((bm, 1), lambda i: (i, 0))],
        out_shape=[jax.ShapeDtypeStruct((s, d), BF16), jax.ShapeDtypeStruct((s, 1), F32)],
        compiler_params=_params(("parallel",), 32),
        name="norm_prep",
    )(x, g.reshape(1, d))


def _final_kernel(h_ref, ssq_ref, g_ref, o_ref, *, d):
    y = h_ref[...] * _row_scale(ssq_ref[...], d)
    o_ref[...] = y * g_ref[...]


def _final_norm(h, ssq, g):
    s, d = h.shape
    bm = _blk(s, 256)
    return pl.pallas_call(
        functools.partial(_final_kernel, d=d),
        grid=(s // bm,),
        in_specs=[pl.BlockSpec((bm, d), lambda i: (i, 0)), pl.BlockSpec((bm, 1), lambda i: (i, 0)),
                  pl.BlockSpec((1, d), lambda i: (0, 0))],
        out_specs=pl.BlockSpec((bm, d), lambda i: (i, 0)),
        out_shape=jax.ShapeDtypeStruct((s, d), F32),
        compiler_params=_params(("parallel",), 32),
        name="final_norm",
    )(h, ssq, g.reshape(1, d))


def _split_w_kernel(w_ref, a_ref, b_ref, *, n, off):
    a_ref[...] = w_ref[:, :n].astype(BF16)
    b_ref[...] = w_ref[:, off:off + n].astype(BF16)


def _split_w_in(w, n, off):
    depth, d, width = w.shape
    assert off + n == width
    br = _blk(d, 256)
    out = lambda: pl.BlockSpec((None, br, n), lambda l, r: (l, r, 0))
    return pl.pallas_call(
        functools.partial(_split_w_kernel, n=n, off=off),
        grid=(depth, d // br),
        in_specs=[pl.BlockSpec((None, br, width), lambda l, r: (l, r, 0))],
        out_specs=[out(), out()],
        out_shape=[jax.ShapeDtypeStruct((depth, d, n), BF16)] * 2,
        compiler_params=_params(("parallel", "parallel"), 40),
        name="split_w_in",
    )(w)


def _in_kernel(x_ref, ssq_ref, w_ref, cs_ref, o_ref, *, d):
    acc = _dot(x_ref[...], w_ref[...].astype(BF16))
    o_ref[...] = (acc * _row_scale(ssq_ref[...], d) * cs_ref[...]).astype(o_ref.dtype)


def _in_proj(hg, ssq, w, layer, col_scale):
    s, d = hg.shape
    n = col_scale.shape[1]
    bm, bn = _blk(s, 1024), _blk(n, 1024 if w.dtype == BF16 else 512)
    return pl.pallas_call(
        functools.partial(_in_kernel, d=d),
        grid=(s // bm, n // bn),
        in_specs=[pl.BlockSpec((bm, d), lambda i, j: (i, 0)), pl.BlockSpec((bm, 1), lambda i, j: (i, 0)),
                  pl.BlockSpec((None, d, bn), lambda i, j: (layer, 0, j)),
                  pl.BlockSpec((1, bn), lambda i, j: (0, j))],
        out_specs=pl.BlockSpec((bm, bn), lambda i, j: (i, j)),
        out_shape=jax.ShapeDtypeStruct((s, n), BF16),
        compiler_params=_params(("parallel", "parallel"), 56),
        name="in_proj",
    )(hg, ssq, w, col_scale)


def _tail_kernel(x_ref, ssq_ref, w_ref, bf_ref, z_ref, f_ref, carry_ref, *, d, rank, bm):
    i = pl.program_id(0)

    @pl.when(i == 0)
    def _():
        carry_ref[...] = jnp.zeros_like(carry_ref)

    acc = _dot(x_ref[...], w_ref[...]) * _row_scale(ssq_ref[...], d)
    z_ref[...] = acc[:, :rank].astype(BF16)
    logit = acc[:, rank:] + bf_ref[...]
    logf = jnp.minimum(logit, 0.0) - jnp.log1p(jnp.exp(-jnp.abs(logit)))
    row = lax.broadcasted_iota(jnp.int32, (bm, bm), 0)
    col = lax.broadcasted_iota(jnp.int32, (bm, bm), 1)
    tri = (col <= row).astype(BF16)
    hi, mid, lo = _split3(logf)
    cs = _dot(tri, hi) + _dot(tri, mid) + _dot(tri, lo) + carry_ref[...]
    carry_ref[...] = cs[bm - 1:bm, :]
    for n, piece in enumerate(_split3(cs * LOG2E)):
        f_ref[:, n * LANES:(n + 1) * LANES] = piece


def _gate_tail(hg, ssq, w_tail, layer, bf_pad, rank):
    s, d = hg.shape
    n = w_tail.shape[2]
    bm = _blk(s, 512)
    return pl.pallas_call(
        functools.partial(_tail_kernel, d=d, rank=rank, bm=bm),
        grid=(s // bm,),
        in_specs=[pl.BlockSpec((bm, d), lambda i: (i, 0)), pl.BlockSpec((bm, 1), lambda i: (i, 0)),
                  pl.BlockSpec((None, d, n), lambda i: (layer, 0, 0)),
                  pl.BlockSpec((None, 1, LANES), lambda i: (layer, 0, 0))],
        out_specs=[pl.BlockSpec((bm, rank), lambda i: (i, 0)),
                   pl.BlockSpec((bm, N_BIAS_PIECES * LANES), lambda i: (i, 0))],
        out_shape=[jax.ShapeDtypeStruct((s, rank), BF16),
                   jax.ShapeDtypeStruct((s, N_BIAS_PIECES * LANES), BF16)],
        scratch_shapes=[pltpu.VMEM((1, LANES), F32)],
        compiler_params=_params(("arbitrary",), 40),
        name="gate_tail",
    )(hg, ssq, w_tail, bf_pad)


def _pool_kernel(x_ref, halo_ref, w_ref, sc_ref, o_ref, *, bm, gw):
    i = pl.program_id(0)
    x = x_ref[...].astype(F32)
    halo = jnp.where(i > 0, halo_ref[...].astype(F32), 0.0)
    xx = jnp.concatenate([halo, x], axis=0)
    pos = i * bm + lax.broadcasted_iota(jnp.int32, (bm, 1), 0)
    for g, w in enumerate(POOL_WINDOWS):
        sl = slice(g * gw, (g + 1) * gw)
        s = xx[:, sl]
        span = 1
        while span < w:
            s = s + pltpu.roll(s, span, axis=0)
            span *= 2
        cnt = jnp.minimum(pos + 1, w).astype(F32)
        pooled = s[POOL_HALO:, :] / cnt - x[:, sl]
        mixed = _dot(pooled.astype(BF16), w_ref[g])
        o_ref[:, sl] = (mixed * sc_ref[:, sl]).astype(BF16)


def _pool_mixer(proj, col_blk, pool_w, layer, pool_scale):
    s = proj.shape[0]
    _, ng, gw, _ = pool_w.shape
    c = ng * gw
    assert all(w & (w - 1) == 0 and w - 1 <= POOL_HALO for w in POOL_WINDOWS) and ng == len(POOL_WINDOWS)
    bm = _blk(s, 512)
    per = bm // POOL_HALO
    return pl.pallas_call(
        functools.partial(_pool_kernel, bm=bm, gw=gw),
        grid=(s // bm,),
        in_specs=[pl.BlockSpec((bm, c), lambda i: (i, col_blk)),
                  pl.BlockSpec((POOL_HALO, c), lambda i: (jnp.maximum(i * per - 1, 0), col_blk)),
                  pl.BlockSpec((None, ng, gw, gw), lambda i: (layer, 0, 0, 0)),
                  pl.BlockSpec((None, 1, c), lambda i: (layer, 0, 0))],
        out_specs=pl.BlockSpec((bm, c), lambda i: (i, 0)),
        out_shape=jax.ShapeDtypeStruct((s, c), BF16),
        compiler_params=_params(("parallel",), 32),
        name="pool_mixer",
    )(proj, proj, pool_w, pool_scale)


def _attn_kernel(first_ref, qt_ref, k_ref, vt_ref, o_ref, st0, st1, p0, p1, acc_ref, *, blk):
    i = pl.program_id(1)
    first = first_ref[pl.program_id(0) * pl.num_programs(1) + i]
    qt = qt_ref[...]

    def scores(j):
        off = pl.multiple_of(j * blk, blk)
        return _dot(k_ref[pl.ds(off, blk), :], qt)

    def softmax(st_ref, p_ref, m, l, masked):
        st = st_ref[...]
        if masked:
            key = lax.broadcasted_iota(jnp.int32, (blk, blk), 0)
            qry = lax.broadcasted_iota(jnp.int32, (blk, blk), 1)
            st = jnp.where(key <= qry, st, -jnp.inf)
        m_new = jnp.maximum(m, jnp.max(st, axis=0, keepdims=True))
        alpha = jnp.exp2(m - m_new)
        p = jnp.exp2(st - m_new)
        p_ref[...] = p.astype(BF16)
        return m_new, alpha * l + jnp.sum(p, axis=0, keepdims=True), alpha

    def pv(p_ref, j):
        return _dot(vt_ref[j], p_ref[...])

    st_bufs, p_bufs = (st0, st1), (p0, p1)

    def group(a, m, l, n, tail):
        for b in range(n):
            last = tail and b == n - 1
            if not last:
                st_bufs[(b + 1) % 2][...] = scores(a + b + 1)
            owed = pv(p_bufs[(b + 1) % 2], jnp.maximum(a + b - 1, 0))
            m, l, alpha = softmax(st_bufs[b % 2], p_bufs[b % 2], m, l, last)
            acc_ref[...] = (acc_ref[...] + owed) * alpha
        return m, l

    p1[...] = jnp.zeros_like(p1)
    acc_ref[...] = jnp.zeros_like(acc_ref)
    st0[...] = scores(first)
    m0 = jnp.full((1, blk), -jnp.inf, F32)
    l0 = jnp.zeros((1, blk), F32)
    trips = (i - first) // ATTN_GROUP
    m, l = lax.fori_loop(0, trips, lambda t, c: group(first + ATTN_GROUP * t, c[0], c[1], ATTN_GROUP, False),
                         (m0, l0))
    rem = i - first + 1 - ATTN_GROUP * trips

    for n in range(1, ATTN_GROUP + 1):
        @pl.when(rem == n)
        def _(n=n):
            _, l_fin = group(i + 1 - n, m, l, n, True)
            out = (acc_ref[...] + pv(p_bufs[(n - 1) % 2], i)) / l_fin
            o_ref[...] = out.T.astype(o_ref.dtype)


def _fox_attention(proj, f_pieces, n_heads):
    s = proj.shape[0]
    c = n_heads * HEAD_DIM
    blk = _blk(s, 512)
    nb = s // blk
    heads = lambda a: a.reshape(s, n_heads, HEAD_DIM)
    pieces = f_pieces.reshape(s, N_BIAS_PIECES, LANES)[:, :, :n_heads].transpose(0, 2, 1)
    ones = jnp.ones_like(pieces)
    zpad = jnp.zeros((s, n_heads, MXU_DEPTH - HEAD_DIM - 2 * N_BIAS_PIECES), BF16)
    q_aug = jnp.concatenate([heads(proj[:, c:2 * c]), pieces, ones, zpad], axis=-1)
    k_aug = jnp.concatenate([heads(proj[:, 2 * c:3 * c]), ones, -pieces, zpad], axis=-1)
    qt = q_aug.transpose(1, 2, 0)
    ka = k_aug.transpose(1, 0, 2)
    vt = proj[:, 3 * c:4 * c].reshape(nb, blk, n_heads, HEAD_DIM).transpose(2, 0, 3, 1)
    first = _first_contributing_block(q_aug, k_aug, pieces, nb, blk)
    grid_spec = pltpu.PrefetchScalarGridSpec(
        num_scalar_prefetch=1,
        grid=(n_heads, nb),
        in_specs=[pl.BlockSpec((None, MXU_DEPTH, blk), lambda h, i, first: (h, 0, i)),
                  pl.BlockSpec((None, s, MXU_DEPTH), lambda h, i, first: (h, 0, 0)),
                  pl.BlockSpec((None, nb, HEAD_DIM, blk), lambda h, i, first: (h, 0, 0, 0))],
        out_specs=pl.BlockSpec((blk, HEAD_DIM), lambda h, i, first: (i, h)),
        scratch_shapes=[pltpu.VMEM((blk, blk), F32), pltpu.VMEM((blk, blk), F32),
                        pltpu.VMEM((blk, blk), BF16), pltpu.VMEM((blk, blk), BF16),
                        pltpu.VMEM((HEAD_DIM, blk), F32)])
    return pl.pallas_call(
        functools.partial(_attn_kernel, blk=blk),
        grid_spec=grid_spec,
        out_shape=jax.ShapeDtypeStruct((s, c), BF16),
        compiler_params=_params(("parallel", "parallel"), 48),
        name="fox_attention",
    )(first, qt, ka, vt)


def _first_contributing_block(q_aug, k_aug, pieces, nb, blk):
    s, n_heads, _ = q_aug.shape
    norm = lambda a: jnp.sqrt(jnp.max(jnp.sum(jnp.square(a[:, :, :HEAD_DIM].astype(F32)), axis=-1), axis=0))
    slack = 2.0 * norm(q_aug) * norm(k_aug) * (1.0 + 2.0 ** -6) + 2.0
    f = jnp.sum(pieces.astype(F32), axis=-1).reshape(nb, blk, n_heads)
    f_max, f_min = jnp.max(f, axis=1), jnp.min(f, axis=1)
    bound = f_max[:, None, :] - f_min[None, :, :] + slack
    needed = bound >= -EXP2_UNDERFLOW
    needed = needed | (jnp.arange(nb)[:, None] == jnp.arange(nb)[None, :])[:, :, None]
    first = jnp.argmax(needed, axis=1).astype(jnp.int32)
    return first.T.reshape(n_heads * nb)


def _layer_norm(x, g, b):
    mu = jnp.mean(x, axis=-1, keepdims=True)
    xc = x - mu
    var = jnp.mean(xc * xc, axis=-1, keepdims=True)
    return xc * lax.rsqrt(var + EPS) * g + b


def _conv_kernel(a_ref, g_ref, ah_ref, gh_ref, cw_ref, cb_ref, lg_ref, lb_ref, o_ref, u_scr, y_scr, *, bm, kw):
    i = pl.program_id(0)
    n_tiles = cw_ref.shape[0]
    u = a_ref[...].astype(F32) * jax.nn.sigmoid(g_ref[...].astype(F32))
    uh = ah_ref[...].astype(F32) * jax.nn.sigmoid(gh_ref[...].astype(F32))
    uh = jnp.where(i > 0, uh, 0.0)
    n_shifted = bm + CONV_HALO - SUBLANES
    for t in range(n_tiles):
        cs = slice(t * LANES, (t + 1) * LANES)
        u_scr[0, t, :CONV_HALO, :] = uh[:, cs]
        u_scr[0, t, CONV_HALO:, :] = u[:, cs]
        for r in range(1, SUBLANES):
            u_scr[r, t, :n_shifted, :] = u_scr[0, t, pl.ds(r, n_shifted), :]

    def tile(t, carry):
        for r0 in range(0, bm, CONV_STRIP):
            y = jnp.zeros((CONV_STRIP, LANES), F32) + cb_ref[t]
            for j in range(kw):
                q, r = divmod(CONV_HALO - (kw - 1) + j, SUBLANES)
                y = y + cw_ref[t, j:j + 1, :] * u_scr[r, t, r0 + SUBLANES * q:r0 + SUBLANES * q + CONV_STRIP, :]
            y_scr[t, r0:r0 + CONV_STRIP, :] = y
        return carry

    lax.fori_loop(0, n_tiles, tile, 0)
    y = jnp.concatenate([y_scr[t] for t in range(n_tiles)], axis=1)
    y = _layer_norm(y, lg_ref[...], lb_ref[...])
    o_ref[...] = (y * jax.nn.sigmoid(y)).astype(BF16)


def _conformer_conv(proj, a_blk, conv_w, conv_b, ln_g, ln_b, layer):
    s = proj.shape[0]
    _, n_tiles, kw, _ = conv_w.shape
    c = n_tiles * LANES
    assert kw - 1 <= CONV_HALO
    bm = _blk(s, 256)
    assert bm % CONV_STRIP == 0
    per = bm // CONV_HALO
    halo_row = lambda i: jnp.maximum(i * per - 1, 0)
    vec = lambda: pl.BlockSpec((None, 1, c), lambda i: (layer, 0, 0))
    return pl.pallas_call(
        functools.partial(_conv_kernel, bm=bm, kw=kw),
        grid=(s // bm,),
        in_specs=[pl.BlockSpec((bm, c), lambda i: (i, a_blk)),
                  pl.BlockSpec((bm, c), lambda i: (i, a_blk + 1)),
                  pl.BlockSpec((CONV_HALO, c), lambda i: (halo_row(i), a_blk)),
                  pl.BlockSpec((CONV_HALO, c), lambda i: (halo_row(i), a_blk + 1)),
                  pl.BlockSpec((None, n_tiles, kw, LANES), lambda i: (layer, 0, 0, 0)),
                  pl.BlockSpec((None, n_tiles, 1, LANES), lambda i: (layer, 0, 0, 0)), vec(), vec()],
        out_specs=pl.BlockSpec((bm, c), lambda i: (i, 0)),
        out_shape=jax.ShapeDtypeStruct((s, c), BF16),
        scratch_shapes=[pltpu.VMEM((SUBLANES, n_tiles, bm + CONV_HALO, LANES), F32),
                        pltpu.VMEM((n_tiles, bm, LANES), F32)],
        compiler_params=_params(("parallel",), 32),
        name="conformer_conv",
    )(proj, proj, proj, proj, conv_w, conv_b, ln_g, ln_b)


def _gelu_tanh(x):
    return x * (0.5 * (1.0 + jnp.tanh(GELU_C0 * (x + GELU_C1 * (x * x * x)))))


def _sgu_kernel(u_ref, v_ref, w_ref, bt_ref, lg_ref, lb_ref, o_ref, *, bm, chunk, gw):
    u = _gelu_tanh(u_ref[...].astype(F32))
    v = _layer_norm(_gelu_tanh(v_ref[...].astype(F32)), lg_ref[...], lb_ref[...]).astype(BF16)
    row = lax.broadcasted_iota(jnp.int32, (chunk, chunk), 0)
    col = lax.broadcasted_iota(jnp.int32, (chunk, chunk), 1)
    for g in range(w_ref.shape[0]):
        wg = jnp.where(col <= row, w_ref[g], 0.0).astype(BF16)
        bias = bt_ref[:, g:g + 1]
        cs = slice(g * gw, (g + 1) * gw)
        for n in range(bm // chunk):
            rs = slice(n * chunk, (n + 1) * chunk)
            mixed = _dot(wg, v[rs, cs]) + bias
            o_ref[rs, cs] = (u[rs, cs] * mixed).astype(BF16)


def _spatial_gating(proj, u_blk, w_s, b_st, ln_g, ln_b, layer):
    s = proj.shape[0]
    _, ng, chunk, _ = w_s.shape
    c = ln_g.shape[-1]
    gw = c // ng
    bm = _blk(s, 4 * chunk)
    vec = lambda: pl.BlockSpec((None, 1, c), lambda i: (layer, 0, 0))
    return pl.pallas_call(
        functools.partial(_sgu_kernel, bm=bm, chunk=chunk, gw=gw),
        grid=(s // bm,),
        in_specs=[pl.BlockSpec((bm, c), lambda i: (i, u_blk)),
                  pl.BlockSpec((bm, c), lambda i: (i, u_blk + 1)),
                  pl.BlockSpec((None, ng, chunk, chunk), lambda i: (layer, 0, 0, 0)),
                  pl.BlockSpec((None, chunk, ng), lambda i: (layer, 0, 0)), vec(), vec()],
        out_specs=pl.BlockSpec((bm, c), lambda i: (i, 0)),
        out_shape=jax.ShapeDtypeStruct((s, c), BF16),
        compiler_params=_params(("parallel",), 32),
        name="spatial_gating",
    )(proj, proj, w_s, b_st, ln_g, ln_b)


def _merge_kernel(b0_ref, b1_ref, b2_ref, b3_ref, z_ref, wb_ref, wg_ref, bg_ref, o_ref):
    z = z_ref[...]
    merged = None
    for b, br in enumerate((b0_ref, b1_ref, b2_ref, b3_ref)):
        gate = jax.nn.sigmoid(_dot(z, wg_ref[b]) + bg_ref[b:b + 1, :])
        term = gate * _dot(br[...], wb_ref[b])
        merged = term if merged is None else merged + term
    o_ref[...] = merged.astype(BF16)


def _merge(branches, z, w_branch, w_gate_up, b_gate, layer):
    s, c = branches[0].shape
    _, nb, rank, d = w_gate_up.shape
    assert nb == 4 and len(branches) == 4
    bm, bn = _blk(s, 512), _blk(d, 1024)
    act = lambda w: pl.BlockSpec((bm, w), lambda j, i: (i, 0))
    return pl.pallas_call(
        _merge_kernel,
        grid=(d // bn, s // bm),
        in_specs=[act(c), act(c), act(c), act(c), act(rank),
                  pl.BlockSpec((None, nb, c, bn), lambda j, i: (layer, 0, 0, j)),
                  pl.BlockSpec((None, nb, rank, bn), lambda j, i: (layer, 0, 0, j)),
                  pl.BlockSpec((None, nb, bn), lambda j, i: (layer, 0, j))],
        out_specs=pl.BlockSpec((bm, bn), lambda j, i: (i, j)),
        out_shape=jax.ShapeDtypeStruct((s, d), BF16),
        compiler_params=_params(("parallel", "parallel"), 56),
        name="branch_merge",
    )(*branches, z, w_branch, w_gate_up, b_gate)


def _emit_resid(hn, j, g_ref, ho_ref, hg_ref, ssq_ref):
    ho_ref[...] = hn
    hg_ref[...] = (hn * g_ref[...]).astype(BF16)
    part = jnp.sum(hn * hn, axis=-1, keepdims=True)

    @pl.when(j == 0)
    def _():
        ssq_ref[...] = part

    @pl.when(j > 0)
    def _():
        ssq_ref[...] += part


def _resid_kernel(x_ref, w_ref, h_ref, g_ref, ho_ref, hg_ref, ssq_ref):
    hn = h_ref[...] + _dot(x_ref[...], w_ref[...])
    _emit_resid(hn, pl.program_id(1), g_ref, ho_ref, hg_ref, ssq_ref)


def _resid_matmul(x, w, layer, h, g, g_layer, bm_pref, bn_pref):
    s, kdim = x.shape
    d = w.shape[2]
    bm, bn = _blk(s, bm_pref), _blk(d, bn_pref)
    return pl.pallas_call(
        _resid_kernel,
        grid=(s // bm, d // bn),
        in_specs=[pl.BlockSpec((bm, kdim), lambda i, j: (i, 0)),
                  pl.BlockSpec((None, kdim, bn), lambda i, j: (layer, 0, j)),
                  pl.BlockSpec((bm, bn), lambda i, j: (i, j)),
                  pl.BlockSpec((None, 1, bn), lambda i, j: (g_layer, 0, j))],
        out_specs=[pl.BlockSpec((bm, bn), lambda i, j: (i, j)),
                   pl.BlockSpec((bm, bn), lambda i, j: (i, j)),
                   pl.BlockSpec((bm, 1), lambda i, j: (i, 0))],
        out_shape=[jax.ShapeDtypeStruct((s, d), F32), jax.ShapeDtypeStruct((s, d), BF16),
                   jax.ShapeDtypeStruct((s, 1), F32)],
        compiler_params=_params(("parallel", "arbitrary"), 56),
        name="resid_matmul",
    )(x, w, h, g)


def _ffn_act_kernel(x_ref, ssq_ref, wg_ref, wu_ref, o_ref, *, d):
    r = _row_scale(ssq_ref[...], d)
    x = x_ref[...]
    a = _dot(x, wg_ref[...]) * r
    u = _dot(x, wu_ref[...]) * r
    o_ref[...] = (a * jax.nn.sigmoid(a) * u).astype(BF16)


def _ffn_act(hg, ssq, w_gate, w_up, layer):
    s, d = hg.shape
    f = w_gate.shape[2]
    bm, bn = _blk(s, 1024), _blk(f, 512)
    wspec = lambda: pl.BlockSpec((None, d, bn), lambda i, j: (layer, 0, j))
    return pl.pallas_call(
        functools.partial(_ffn_act_kernel, d=d),
        grid=(s // bm, f // bn),
        in_specs=[pl.BlockSpec((bm, d), lambda i, j: (i, 0)), pl.BlockSpec((bm, 1), lambda i, j: (i, 0)),
                  wspec(), wspec()],
        out_specs=pl.BlockSpec((bm, bn), lambda i, j: (i, j)),
        out_shape=jax.ShapeDtypeStruct((s, f), BF16),
        compiler_params=_params(("parallel", "parallel"), 56),
        name="ffn_act",
    )(hg, ssq, w_gate, w_up)


def _ple_kernel(hg_ref, ssq_ref, wd_ref, wu_ref, p_ref, wp_ref, h_ref, g_ref,
                ho_ref, xg_ref, ssqo_ref, t_ref, *, d):
    j = pl.program_id(1)

    @pl.when(j == 0)
    def _():
        t = _dot(hg_ref[...], wd_ref[...]) * _row_scale(ssq_ref[...], d)
        t_ref[...] = t.astype(BF16)

    gate = jax.nn.sigmoid(_dot(t_ref[...], wu_ref[...]))
    inj = _dot(p_ref[...].astype(BF16), wp_ref[...])
    _emit_resid(h_ref[...] + gate * inj, j, g_ref, ho_ref, xg_ref, ssqo_ref)


def _ple(hg, ssq, w_down, w_up, p, w_ple, layer, h, g_next):
    s, d = hg.shape
    rank = w_down.shape[2]
    pdim = p.shape[2]
    bm, bn = _blk(s, 1024), _blk(d, 1024)
    return pl.pallas_call(
        functools.partial(_ple_kernel, d=d),
        grid=(s // bm, d // bn),
        in_specs=[pl.BlockSpec((bm, d), lambda i, j: (i, 0)), pl.BlockSpec((bm, 1), lambda i, j: (i, 0)),
                  pl.BlockSpec((None, d, rank), lambda i, j: (layer, 0, 0)),
                  pl.BlockSpec((None, rank, bn), lambda i, j: (layer, 0, j)),
                  pl.BlockSpec((None, bm, pdim), lambda i, j: (layer, i, 0)),
                  pl.BlockSpec((None, pdim, bn), lambda i, j: (layer, 0, j)),
                  pl.BlockSpec((bm, bn), lambda i, j: (i, j)),
                  pl.BlockSpec((1, bn), lambda i, j: (0, j))],
        out_specs=[pl.BlockSpec((bm, bn), lambda i, j: (i, j)),
                   pl.BlockSpec((bm, bn), lambda i, j: (i, j)),
                   pl.BlockSpec((bm, 1), lambda i, j: (i, 0))],
        out_shape=[jax.ShapeDtypeStruct((s, d), F32), jax.ShapeDtypeStruct((s, d), BF16),
                   jax.ShapeDtypeStruct((s, 1), F32)],
        scratch_shapes=[pltpu.VMEM((bm, rank), BF16)],
        compiler_params=_params(("parallel", "arbitrary"), 56),
        name="per_layer_input",
    )(hg, ssq, w_down, w_up, p, w_ple, h, g_next.reshape(1, d))


def kernel(x, p, norm_mix, w_in, b_f, pool_w, pool_scale, conv_w, conv_b, conv_ln_g, conv_ln_b, sgu_ln_g, sgu_ln_b, sgu_w, sgu_b, w_branch, w_gate_down, w_gate_up, b_gate, w_out, norm_ffn, w_ffn_gate, w_ffn_up, w_ffn_down, norm_ple, w_ple_gate_down, w_ple_gate_up, w_ple, norm_final):
    batch, s, d = x.shape
    assert batch == 1
    depth = w_in.shape[0]
    c = pool_scale.shape[1]
    n_heads = b_f.shape[1]
    rank = w_gate_down.shape[2]
    assert c == n_heads * HEAD_DIM and w_in.shape[2] == 8 * c + n_heads
    off_f = 4 * c

    bf = lambda a: a.astype(BF16)
    w_qkv, w_cs = _split_w_in(w_in, off_f, off_f + n_heads)
    f_cols = jnp.pad(w_in[:, :, off_f:off_f + n_heads], ((0, 0), (0, 0), (0, LANES - n_heads)))
    w_tail = bf(jnp.concatenate([w_gate_down, f_cols], axis=2))
    vec3 = lambda a: a.reshape(depth, 1, a.shape[-1])
    bf_pad = vec3(jnp.pad(b_f, ((0, 0), (0, LANES - n_heads))))
    pool_scale, conv_ln_g, conv_ln_b = vec3(pool_scale), vec3(conv_ln_g), vec3(conv_ln_b)
    kw = conv_w.shape[1]
    conv_w = conv_w.reshape(depth, kw, c // LANES, LANES).transpose(0, 2, 1, 3)
    conv_b = conv_b.reshape(depth, c // LANES, 1, LANES)
    sgu_ln_g, sgu_ln_b, norm_ffn, norm_ple = vec3(sgu_ln_g), vec3(sgu_ln_b), vec3(norm_ffn), vec3(norm_ple)
    ones4c = jnp.ones((1, 4 * c), F32)
    qkv_scale = ones4c.at[:, c:2 * c].set(HEAD_DIM ** -0.5 * LOG2E)
    pool_wb, w_branch_b, w_gate_up_b, w_out_b = bf(pool_w), bf(w_branch), bf(w_gate_up), bf(w_out)
    w_fg, w_fu, w_fd = bf(w_ffn_gate), bf(w_ffn_up), bf(w_ffn_down)
    w_pd, w_pu, w_pl = bf(w_ple_gate_down), bf(w_ple_gate_up), bf(w_ple)
    sgu_bt = sgu_b.transpose(0, 2, 1)
    p3 = p.reshape(depth, s, p.shape[-1])

    h = x.reshape(s, d)
    hg, ssq = _norm_prep(h, norm_mix[0])
    for i in range(depth):
        qkv = _in_proj(hg, ssq, w_qkv, i, qkv_scale)
        cs = _in_proj(hg, ssq, w_cs, i, ones4c)
        z, f_pieces = _gate_tail(hg, ssq, w_tail, i, bf_pad, rank)
        branches = (
            _pool_mixer(qkv, 0, pool_wb, i, pool_scale),
            _fox_attention(qkv, f_pieces, n_heads),
            _conformer_conv(cs, 0, conv_w, conv_b, conv_ln_g, conv_ln_b, i),
            _spatial_gating(cs, 2, sgu_w, sgu_bt, sgu_ln_g, sgu_ln_b, i),
        )
        merged = _merge(branches, z, w_branch_b, w_gate_up_b, b_gate, i)
        h, hg, ssq = _resid_matmul(merged, w_out_b, i, h, norm_ffn, i, 1024, 512)

        act = _ffn_act(hg, ssq, w_fg, w_fu, i)
        h, hg, ssq = _resid_matmul(act, w_fd, i, h, norm_ple, i, 512, 512)

        g_next = norm_mix[i + 1] if i + 1 < depth else norm_final
        h, hg, ssq = _ple(hg, ssq, w_pd, w_pu, p3, w_pl, i, h, g_next)
    return _final_norm(h, ssq, norm_final).reshape(batch, s, d)
```

```python
import functools
import math

import jax
import jax.numpy as jnp
from jax import lax
from jax.experimental import pallas as pl
from jax.experimental.pallas import tpu as pltpu

F32 = jnp.float32
BF16 = jnp.bfloat16

EPS = 1e-6
HEAD_DIM = 128
POOL_WINDOWS = (2, 4, 8, 16)
GELU_C0 = math.sqrt(2.0 / math.pi)
GELU_C1 = 0.044715
LOG2E = math.log2(math.e)

LANES = 128
MXU_DEPTH = 256
MIB = 1024 * 1024

POOL_HALO = 16
CONV_HALO = 32
N_BIAS_PIECES = 3
ATTN_GROUP = 4
EXP2_UNDERFLOW = 160.0
CONV_STRIP = 128
SUBLANES = 8


def _params(semantics, vmem_mib):
    return pltpu.CompilerParams(dimension_semantics=semantics, vmem_limit_bytes=vmem_mib * MIB)


def _blk(n, pref):
    b = min(n, pref)
    while n % b:
        b -= LANES
    assert b > 0
    return b


def _row_scale(ssq, d):
    return lax.rsqrt(ssq / d + EPS)


def _dot(a, b):
    return jnp.dot(a, b, preferred_element_type=F32)


def _split3(x):
    hi = x.astype(BF16)
    rem = x - hi.astype(F32)
    mid = rem.astype(BF16)
    lo = (rem - mid.astype(F32)).astype(BF16)
    return hi, mid, lo


def _prep_kernel(x_ref, g_ref, hg_ref, ssq_ref):
    x = x_ref[...]
    ssq_ref[...] = jnp.sum(x * x, axis=-1, keepdims=True)
    hg_ref[...] = (x * g_ref[...]).astype(BF16)


def _norm_prep(x, g):
    s, d = x.shape
    bm = _blk(s, 256)
    return pl.pallas_call(
        _prep_kernel,
        grid=(s // bm,),
        in_specs=[pl.BlockSpec((bm, d), lambda i: (i, 0)), pl.BlockSpec((1, d), lambda i: (0, 0))],
        out_specs=[pl.BlockSpec((bm, d), lambda i: (i, 0)), pl.BlockSpec((bm, 1), lambda i: (i, 0))],
        out_shape=[jax.ShapeDtypeStruct((s, d), BF16), jax.ShapeDtypeStruct((s, 1), F32)],
        compiler_params=_params(("parallel",), 32),
        name="norm_prep",
    )(x, g.reshape(1, d))


def _final_kernel(h_ref, ssq_ref, g_ref, o_ref, *, d):
    y = h_ref[...] * _row_scale(ssq_ref[...], d)
    o_ref[...] = y * g_ref[...]


def _final_norm(h, ssq, g):
    s, d = h.shape
    bm = _blk(s, 256)
    return pl.pallas_call(
        functools.partial(_final_kernel, d=d),
        grid=(s // bm,),
        in_specs=[pl.BlockSpec((bm, d), lambda i: (i, 0)), pl.BlockSpec((bm, 1), lambda i: (i, 0)),
                  pl.BlockSpec((1, d), lambda i: (0, 0))],
        out_specs=pl.BlockSpec((bm, d), lambda i: (i, 0)),
        out_shape=jax.ShapeDtypeStruct((s, d), F32),
        compiler_params=_params(("parallel",), 32),
        name="final_norm",
    )(h, ssq, g.reshape(1, d))


def _split_w_kernel(a_ref, b_ref, ao_ref, bo_ref):
    ao_ref[...] = a_ref[...].astype(BF16)
    bo_ref[...] = b_ref[...].astype(BF16)


def _split_w_in(w, n, off):
    depth, d, width = w.shape
    assert off + n == width and off % SUBLANES == 0
    wt = jnp.swapaxes(w, 1, 2)
    br = _blk(n, 256)
    out = lambda: pl.BlockSpec((None, br, d), lambda l, r: (l, r, 0))
    return pl.pallas_call(
        _split_w_kernel,
        grid=(depth, n // br),
        in_specs=[pl.BlockSpec((None, br, d), lambda l, r: (l, r, 0)),
                  pl.BlockSpec((pl.Squeezed(), pl.Element(br), pl.Element(d)), lambda l, r: (l, pl.multiple_of(off + r * br, SUBLANES), 0))],
        out_specs=[out(), out()],
        out_shape=[jax.ShapeDtypeStruct((depth, n, d), BF16)] * 2,
        compiler_params=_params(("parallel", "parallel"), 40),
        name="split_w_in",
    )(wt, wt)


def _in_kernel(x_ref, ssq_ref, wt_ref, cs_ref, o_ref, *, d):
    acc = lax.dot_general(x_ref[...], wt_ref[...], (((1,), (1,)), ((), ())), preferred_element_type=F32)
    o_ref[...] = (acc * _row_scale(ssq_ref[...], d) * cs_ref[...]).astype(o_ref.dtype)


def _in_proj(hg, ssq, wt, layer, col_scale):
    s, d = hg.shape
    n = wt.shape[1]
    bm, bn = _blk(s, 1024), _blk(n, 1024)
    return pl.pallas_call(
        functools.partial(_in_kernel, d=d),
        grid=(s // bm, n // bn),
        in_specs=[pl.BlockSpec((bm, d), lambda i, j: (i, 0)), pl.BlockSpec((bm, 1), lambda i, j: (i, 0)),
                  pl.BlockSpec((None, bn, d), lambda i, j: (layer, j, 0)),
                  pl.BlockSpec((1, bn), lambda i, j: (0, j))],
        out_specs=pl.BlockSpec((bm, bn), lambda i, j: (i, j)),
        out_shape=jax.ShapeDtypeStruct((s, n), BF16),
        compiler_params=_params(("parallel", "parallel"), 56),
        name="in_proj",
    )(hg, ssq, wt, col_scale)


def _tail_kernel(x_ref, ssq_ref, w_ref, bf_ref, z_ref, f_ref, carry_ref, *, d, rank, bm):
    i = pl.program_id(0)

    @pl.when(i == 0)
    def _():
        carry_ref[...] = jnp.zeros_like(carry_ref)

    acc = _dot(x_ref[...], w_ref[...]) * _row_scale(ssq_ref[...], d)
    z_ref[...] = acc[:, :rank].astype(BF16)
    logit = acc[:, rank:] + bf_ref[...]
    logf = jnp.minimum(logit, 0.0) - jnp.log1p(jnp.exp(-jnp.abs(logit)))
    row = lax.broadcasted_iota(jnp.int32, (bm, bm), 0)
    col = lax.broadcasted_iota(jnp.int32, (bm, bm), 1)
    tri = (col <= row).astype(BF16)
    hi, mid, lo = _split3(logf)
    cs = _dot(tri, hi) + _dot(tri, mid) + _dot(tri, lo) + carry_ref[...]
    carry_ref[...] = cs[bm - 1:bm, :]
    for n, piece in enumerate(_split3(cs * LOG2E)):
        f_ref[:, n * LANES:(n + 1) * LANES] = piece


def _gate_tail(hg, ssq, w_tail, layer, bf_pad, rank):
    s, d = hg.shape
    n = w_tail.shape[2]
    bm = _blk(s, 512)
    return pl.pallas_call(
        functools.partial(_tail_kernel, d=d, rank=rank, bm=bm),
        grid=(s // bm,),
        in_specs=[pl.BlockSpec((bm, d), lambda i: (i, 0)), pl.BlockSpec((bm, 1), lambda i: (i, 0)),
                  pl.BlockSpec((None, d, n), lambda i: (layer, 0, 0)),
                  pl.BlockSpec((None, 1, LANES), lambda i: (layer, 0, 0))],
        out_specs=[pl.BlockSpec((bm, rank), lambda i: (i, 0)),
                   pl.BlockSpec((bm, N_BIAS_PIECES * LANES), lambda i: (i, 0))],
        out_shape=[jax.ShapeDtypeStruct((s, rank), BF16),
                   jax.ShapeDtypeStruct((s, N_BIAS_PIECES * LANES), BF16)],
        scratch_shapes=[pltpu.VMEM((1, LANES), F32)],
        compiler_params=_params(("arbitrary",), 40),
        name="gate_tail",
    )(hg, ssq, w_tail, bf_pad)


def _pool_kernel(x_ref, halo_ref, w_ref, sc_ref, o_ref, *, bm, gw):
    i = pl.program_id(0)
    x = x_ref[...].astype(F32)
    halo = jnp.where(i > 0, halo_ref[...].astype(F32), 0.0)
    xx = jnp.concatenate([halo, x], axis=0)
    pos = i * bm + lax.broadcasted_iota(jnp.int32, (bm, 1), 0)
    for g, w in enumerate(POOL_WINDOWS):
        sl = slice(g * gw, (g + 1) * gw)
        s = xx[:, sl]
        span = 1
        while span < w:
            s = s + pltpu.roll(s, span, axis=0)
            span *= 2
        cnt = jnp.minimum(pos + 1, w).astype(F32)
        pooled = s[POOL_HALO:, :] / cnt - x[:, sl]
        mixed = _dot(pooled.astype(BF16), w_ref[g])
        o_ref[:, sl] = (mixed * sc_ref[:, sl]).astype(BF16)


def _pool_mixer(proj, col_blk, pool_w, layer, pool_scale):
    s = proj.shape[0]
    _, ng, gw, _ = pool_w.shape
    c = ng * gw
    assert all(w & (w - 1) == 0 and w - 1 <= POOL_HALO for w in POOL_WINDOWS) and ng == len(POOL_WINDOWS)
    bm = _blk(s, 512)
    per = bm // POOL_HALO
    return pl.pallas_call(
        functools.partial(_pool_kernel, bm=bm, gw=gw),
        grid=(s // bm,),
        in_specs=[pl.BlockSpec((bm, c), lambda i: (i, col_blk)),
                  pl.BlockSpec((POOL_HALO, c), lambda i: (jnp.maximum(i * per - 1, 0), col_blk)),
                  pl.BlockSpec((None, ng, gw, gw), lambda i: (layer, 0, 0, 0)),
                  pl.BlockSpec((None, 1, c), lambda i: (layer, 0, 0))],
        out_specs=pl.BlockSpec((bm, c), lambda i: (i, 0)),
        out_shape=jax.ShapeDtypeStruct((s, c), BF16),
        compiler_params=_params(("parallel",), 32),
        name="pool_mixer",
    )(proj, proj, pool_w, pool_scale)


def _attn_kernel(first_ref, qt_ref, k_ref, vt_ref, o_ref, st0, st1, p0, p1, acc_ref, *, blk):
    i = pl.program_id(1)
    first = first_ref[pl.program_id(0) * pl.num_programs(1) + i]
    qt = qt_ref[...]

    def scores(j):
        off = pl.multiple_of(j * blk, blk)
        return _dot(k_ref[pl.ds(off, blk), :], qt)

    def softmax(st_ref, p_ref, m, l, masked):
        st = st_ref[...]
        if masked:
            key = lax.broadcasted_iota(jnp.int32, (blk, blk), 0)
            qry = lax.broadcasted_iota(jnp.int32, (blk, blk), 1)
            st = jnp.where(key <= qry, st, -jnp.inf)
        m_new = jnp.maximum(m, jnp.max(st, axis=0, keepdims=True))
        alpha = jnp.exp2(m - m_new)
        p = jnp.exp2(st - m_new)
        p_ref[...] = p.astype(BF16)
        return m_new, alpha * l + jnp.sum(p, axis=0, keepdims=True), alpha

    def pv(p_ref, j):
        return _dot(vt_ref[j], p_ref[...])

    st_bufs, p_bufs = (st0, st1), (p0, p1)

    def group(a, m, l, n, tail):
        for b in range(n):
            last = tail and b == n - 1
            if not last:
                st_bufs[(b + 1) % 2][...] = scores(a + b + 1)
            owed = pv(p_bufs[(b + 1) % 2], jnp.maximum(a + b - 1, 0))
            m, l, alpha = softmax(st_bufs[b % 2], p_bufs[b % 2], m, l, last)
            acc_ref[...] = (acc_ref[...] + owed) * alpha
        return m, l

    p1[...] = jnp.zeros_like(p1)
    acc_ref[...] = jnp.zeros_like(acc_ref)
    st0[...] = scores(first)
    m0 = jnp.full((1, blk), -jnp.inf, F32)
    l0 = jnp.zeros((1, blk), F32)
    trips = (i - first) // ATTN_GROUP
    m, l = lax.fori_loop(0, trips, lambda t, c: group(first + ATTN_GROUP * t, c[0], c[1], ATTN_GROUP, False),
                         (m0, l0))
    rem = i - first + 1 - ATTN_GROUP * trips

    for n in range(1, ATTN_GROUP + 1):
        @pl.when(rem == n)
        def _(n=n):
            _, l_fin = group(i + 1 - n, m, l, n, True)
            out = (acc_ref[...] + pv(p_bufs[(n - 1) % 2], i)) / l_fin
            o_ref[...] = out.T.astype(o_ref.dtype)


def _fox_attention(proj, f_pieces, n_heads):
    s = proj.shape[0]
    c = n_heads * HEAD_DIM
    blk = _blk(s, 512)
    nb = s // blk
    heads = lambda a: a.reshape(s, n_heads, HEAD_DIM)
    pieces = f_pieces.reshape(s, N_BIAS_PIECES, LANES)[:, :, :n_heads].transpose(0, 2, 1)
    ones = jnp.ones_like(pieces)
    zpad = jnp.zeros((s, n_heads, MXU_DEPTH - HEAD_DIM - 2 * N_BIAS_PIECES), BF16)
    q_aug = jnp.concatenate([heads(proj[:, c:2 * c]), pieces, ones, zpad], axis=-1)
    k_aug = jnp.concatenate([heads(proj[:, 2 * c:3 * c]), ones, -pieces, zpad], axis=-1)
    qt = q_aug.transpose(1, 2, 0)
    ka = k_aug.transpose(1, 0, 2)
    vt = proj[:, 3 * c:4 * c].reshape(nb, blk, n_heads, HEAD_DIM).transpose(2, 0, 3, 1)
    first = _first_contributing_block(q_aug, k_aug, pieces, nb, blk)
    grid_spec = pltpu.PrefetchScalarGridSpec(
        num_scalar_prefetch=1,
        grid=(n_heads, nb),
        in_specs=[pl.BlockSpec((None, MXU_DEPTH, blk), lambda h, i, first: (h, 0, i)),
                  pl.BlockSpec((None, s, MXU_DEPTH), lambda h, i, first: (h, 0, 0)),
                  pl.BlockSpec((None, nb, HEAD_DIM, blk), lambda h, i, first: (h, 0, 0, 0))],
        out_specs=pl.BlockSpec((blk, HEAD_DIM), lambda h, i, first: (i, h)),
        scratch_shapes=[pltpu.VMEM((blk, blk), F32), pltpu.VMEM((blk, blk), F32),
                        pltpu.VMEM((blk, blk), BF16), pltpu.VMEM((blk, blk), BF16),
                        pltpu.VMEM((HEAD_DIM, blk), F32)])
    return pl.pallas_call(
        functools.partial(_attn_kernel, blk=blk),
        grid_spec=grid_spec,
        out_shape=jax.ShapeDtypeStruct((s, c), BF16),
        compiler_params=_params(("parallel", "parallel"), 48),
        name="fox_attention",
    )(first, qt, ka, vt)


def _first_contributing_block(q_aug, k_aug, pieces, nb, blk):
    s, n_heads, _ = q_aug.shape
    norm = lambda a: jnp.sqrt(jnp.max(jnp.sum(jnp.square(a[:, :, :HEAD_DIM].astype(F32)), axis=-1), axis=0))
    slack = 2.0 * norm(q_aug) * norm(k_aug) * (1.0 + 2.0 ** -6) + 2.0
    f = jnp.sum(pieces.astype(F32), axis=-1).reshape(nb, blk, n_heads)
    f_max, f_min = jnp.max(f, axis=1), jnp.min(f, axis=1)
    bound = f_max[:, None, :] - f_min[None, :, :] + slack
    needed = bound >= -EXP2_UNDERFLOW
    needed = needed | (jnp.arange(nb)[:, None] == jnp.arange(nb)[None, :])[:, :, None]
    first = jnp.argmax(needed, axis=1).astype(jnp.int32)
    return first.T.reshape(n_heads * nb)


def _layer_norm(x, g, b):
    mu = jnp.mean(x, axis=-1, keepdims=True)
    xc = x - mu
    var = jnp.mean(xc * xc, axis=-1, keepdims=True)
    return xc * lax.rsqrt(var + EPS) * g + b


def _conv_kernel(a_ref, g_ref, ah_ref, gh_ref, cw_ref, cb_ref, lg_ref, lb_ref, o_ref, u_scr, y_scr, *, bm, kw):
    i = pl.program_id(0)
    n_tiles = cw_ref.shape[0]
    u = a_ref[...].astype(F32) * jax.nn.sigmoid(g_ref[...].astype(F32))
    uh = ah_ref[...].astype(F32) * jax.nn.sigmoid(gh_ref[...].astype(F32))
    uh = jnp.where(i > 0, uh, 0.0)
    n_shifted = bm + CONV_HALO - SUBLANES
    for t in range(n_tiles):
        cs = slice(t * LANES, (t + 1) * LANES)
        u_scr[0, t, :CONV_HALO, :] = uh[:, cs]
        u_scr[0, t, CONV_HALO:, :] = u[:, cs]
        for r in range(1, SUBLANES):
            u_scr[r, t, :n_shifted, :] = u_scr[0, t, pl.ds(r, n_shifted), :]

    def tile(t, carry):
        for r0 in range(0, bm, CONV_STRIP):
            y = jnp.zeros((CONV_STRIP, LANES), F32) + cb_ref[t]
            for j in range(kw):
                q, r = divmod(CONV_HALO - (kw - 1) + j, SUBLANES)
                y = y + cw_ref[t, j:j + 1, :] * u_scr[r, t, r0 + SUBLANES * q:r0 + SUBLANES * q + CONV_STRIP, :]
            y_scr[t, r0:r0 + CONV_STRIP, :] = y
        return carry

    lax.fori_loop(0, n_tiles, tile, 0)
    y = jnp.concatenate([y_scr[t] for t in range(n_tiles)], axis=1)
    y = _layer_norm(y, lg_ref[...], lb_ref[...])
    o_ref[...] = (y * jax.nn.sigmoid(y)).astype(BF16)


def _conformer_conv(proj, a_blk, conv_w, conv_b, ln_g, ln_b, layer):
    s = proj.shape[0]
    _, n_tiles, kw, _ = conv_w.shape
    c = n_tiles * LANES
    assert kw - 1 <= CONV_HALO
    bm = _blk(s, 256)
    assert bm % CONV_STRIP == 0
    per = bm // CONV_HALO
    halo_row = lambda i: jnp.maximum(i * per - 1, 0)
    vec = lambda: pl.BlockSpec((None, 1, c), lambda i: (layer, 0, 0))
    return pl.pallas_call(
        functools.partial(_conv_kernel, bm=bm, kw=kw),
        grid=(s // bm,),
        in_specs=[pl.BlockSpec((bm, c), lambda i: (i, a_blk)),
                  pl.BlockSpec((bm, c), lambda i: (i, a_blk + 1)),
                  pl.BlockSpec((CONV_HALO, c), lambda i: (halo_row(i), a_blk)),
                  pl.BlockSpec((CONV_HALO, c), lambda i: (halo_row(i), a_blk + 1)),
                  pl.BlockSpec((None, n_tiles, kw, LANES), lambda i: (layer, 0, 0, 0)),
                  pl.BlockSpec((None, n_tiles, 1, LANES), lambda i: (layer, 0, 0, 0)), vec(), vec()],
        out_specs=pl.BlockSpec((bm, c), lambda i: (i, 0)),
        out_shape=jax.ShapeDtypeStruct((s, c), BF16),
        scratch_shapes=[pltpu.VMEM((SUBLANES, n_tiles, bm + CONV_HALO, LANES), F32),
                        pltpu.VMEM((n_tiles, bm, LANES), F32)],
        compiler_params=_params(("parallel",), 32),
        name="conformer_conv",
    )(proj, proj, proj, proj, conv_w, conv_b, ln_g, ln_b)


def _gelu_tanh(x):
    return x * (0.5 * (1.0 + jnp.tanh(GELU_C0 * (x + GELU_C1 * (x * x * x)))))


def _sgu_kernel(u_ref, v_ref, w_ref, bt_ref, lg_ref, lb_ref, o_ref, *, bm, chunk, gw):
    u = _gelu_tanh(u_ref[...].astype(F32))
    v = _layer_norm(_gelu_tanh(v_ref[...].astype(F32)), lg_ref[...], lb_ref[...]).astype(BF16)
    row = lax.broadcasted_iota(jnp.int32, (chunk, chunk), 0)
    col = lax.broadcasted_iota(jnp.int32, (chunk, chunk), 1)
    for g in range(w_ref.shape[0]):
        wg = jnp.where(col <= row, w_ref[g], 0.0).astype(BF16)
        bias = bt_ref[:, g:g + 1]
        cs = slice(g * gw, (g + 1) * gw)
        for n in range(bm // chunk):
            rs = slice(n * chunk, (n + 1) * chunk)
            mixed = _dot(wg, v[rs, cs]) + bias
            o_ref[rs, cs] = (u[rs, cs] * mixed).astype(BF16)


def _spatial_gating(proj, u_blk, w_s, b_st, ln_g, ln_b, layer):
    s = proj.shape[0]
    _, ng, chunk, _ = w_s.shape
    c = ln_g.shape[-1]
    gw = c // ng
    bm = _blk(s, 4 * chunk)
    vec = lambda: pl.BlockSpec((None, 1, c), lambda i: (layer, 0, 0))
    return pl.pallas_call(
        functools.partial(_sgu_kernel, bm=bm, chunk=chunk, gw=gw),
        grid=(s // bm,),
        in_specs=[pl.BlockSpec((bm, c), lambda i: (i, u_blk)),
                  pl.BlockSpec((bm, c), lambda i: (i, u_blk + 1)),
                  pl.BlockSpec((None, ng, chunk, chunk), lambda i: (layer, 0, 0, 0)),
                  pl.BlockSpec((None, chunk, ng), lambda i: (layer, 0, 0)), vec(), vec()],
        out_specs=pl.BlockSpec((bm, c), lambda i: (i, 0)),
        out_shape=jax.ShapeDtypeStruct((s, c), BF16),
        compiler_params=_params(("parallel",), 32),
        name="spatial_gating",
    )(proj, proj, w_s, b_st, ln_g, ln_b)


def _merge_kernel(b0_ref, b1_ref, b2_ref, b3_ref, z_ref, wb_ref, wg_ref, bg_ref, o_ref):
    z = z_ref[...]
    merged = None
    for b, br in enumerate((b0_ref, b1_ref, b2_ref, b3_ref)):
        gate = jax.nn.sigmoid(_dot(z, wg_ref[b]) + bg_ref[b:b + 1, :])
        term = gate * _dot(br[...], wb_ref[b])
        merged = term if merged is None else merged + term
    o_ref[...] = merged.astype(BF16)


def _merge(branches, z, w_branch, w_gate_up, b_gate, layer):
    s, c = branches[0].shape
    _, nb, rank, d = w_gate_up.shape
    assert nb == 4 and len(branches) == 4
    bm, bn = _blk(s, 512), _blk(d, 1024)
    act = lambda w: pl.BlockSpec((bm, w), lambda j, i: (i, 0))
    return pl.pallas_call(
        _merge_kernel,
        grid=(d // bn, s // bm),
        in_specs=[act(c), act(c), act(c), act(c), act(rank),
                  pl.BlockSpec((None, nb, c, bn), lambda j, i: (layer, 0, 0, j)),
                  pl.BlockSpec((None, nb, rank, bn), lambda j, i: (layer, 0, 0, j)),
                  pl.BlockSpec((None, nb, bn), lambda j, i: (layer, 0, j))],
        out_specs=pl.BlockSpec((bm, bn), lambda j, i: (i, j)),
        out_shape=jax.ShapeDtypeStruct((s, d), BF16),
        compiler_params=_params(("parallel", "parallel"), 56),
        name="branch_merge",
    )(*branches, z, w_branch, w_gate_up, b_gate)


def _emit_resid(hn, j, g_ref, ho_ref, hg_ref, ssq_ref):
    ho_ref[...] = hn
    hg_ref[...] = (hn * g_ref[...]).astype(BF16)
    part = jnp.sum(hn * hn, axis=-1, keepdims=True)

    @pl.when(j == 0)
    def _():
        ssq_ref[...] = part

    @pl.when(j > 0)
    def _():
        ssq_ref[...] += part


def _resid_kernel(x_ref, w_ref, h_ref, g_ref, ho_ref, hg_ref, ssq_ref):
    hn = h_ref[...] + _dot(x_ref[...], w_ref[...])
    _emit_resid(hn, pl.program_id(1), g_ref, ho_ref, hg_ref, ssq_ref)


def _resid_matmul(x, w, layer, h, g, g_layer, bm_pref, bn_pref):
    s, kdim = x.shape
    d = w.shape[2]
    bm, bn = _blk(s, bm_pref), _blk(d, bn_pref)
    return pl.pallas_call(
        _resid_kernel,
        grid=(s // bm, d // bn),
        in_specs=[pl.BlockSpec((bm, kdim), lambda i, j: (i, 0)),
                  pl.BlockSpec((None, kdim, bn), lambda i, j: (layer, 0, j)),
                  pl.BlockSpec((bm, bn), lambda i, j: (i, j)),
                  pl.BlockSpec((None, 1, bn), lambda i, j: (g_layer, 0, j))],
        out_specs=[pl.BlockSpec((bm, bn), lambda i, j: (i, j)),
                   pl.BlockSpec((bm, bn), lambda i, j: (i, j)),
                   pl.BlockSpec((bm, 1), lambda i, j: (i, 0))],
        out_shape=[jax.ShapeDtypeStruct((s, d), F32), jax.ShapeDtypeStruct((s, d), BF16),
                   jax.ShapeDtypeStruct((s, 1), F32)],
        compiler_params=_params(("parallel", "arbitrary"), 56),
        name="resid_matmul",
    )(x, w, h, g)


def _ffn_act_kernel(x_ref, ssq_ref, wg_ref, wu_ref, o_ref, *, d):
    r = _row_scale(ssq_ref[...], d)
    x = x_ref[...]
    a = _dot(x, wg_ref[...]) * r
    u = _dot(x, wu_ref[...]) * r
    o_ref[...] = (a * jax.nn.sigmoid(a) * u).astype(BF16)


def _ffn_act(hg, ssq, w_gate, w_up, layer):
    s, d = hg.shape
    f = w_gate.shape[2]
    bm, bn = _blk(s, 1024), _blk(f, 512)
    wspec = lambda: pl.BlockSpec((None, d, bn), lambda i, j: (layer, 0, j))
    return pl.pallas_call(
        functools.partial(_ffn_act_kernel, d=d),
        grid=(s // bm, f // bn),
        in_specs=[pl.BlockSpec((bm, d), lambda i, j: (i, 0)), pl.BlockSpec((bm, 1), lambda i, j: (i, 0)),
                  wspec(), wspec()],
        out_specs=pl.BlockSpec((bm, bn), lambda i, j: (i, j)),
        out_shape=jax.ShapeDtypeStruct((s, f), BF16),
        compiler_params=_params(("parallel", "parallel"), 56),
        name="ffn_act",
    )(hg, ssq, w_gate, w_up)


def _ple_kernel(hg_ref, ssq_ref, wd_ref, wu_ref, p_ref, wp_ref, h_ref, g_ref,
                ho_ref, xg_ref, ssqo_ref, t_ref, *, d):
    j = pl.program_id(1)

    @pl.when(j == 0)
    def _():
        t = _dot(hg_ref[...], wd_ref[...]) * _row_scale(ssq_ref[...], d)
        t_ref[...] = t.astype(BF16)

    gate = jax.nn.sigmoid(_dot(t_ref[...], wu_ref[...]))
    inj = _dot(p_ref[...].astype(BF16), wp_ref[...])
    _emit_resid(h_ref[...] + gate * inj, j, g_ref, ho_ref, xg_ref, ssqo_ref)


def _ple(hg, ssq, w_down, w_up, p, w_ple, layer, h, g_next):
    s, d = hg.shape
    rank = w_down.shape[2]
    pdim = p.shape[2]
    bm, bn = _blk(s, 1024), _blk(d, 1024)
    return pl.pallas_call(
        functools.partial(_ple_kernel, d=d),
        grid=(s // bm, d // bn),
        in_specs=[pl.BlockSpec((bm, d), lambda i, j: (i, 0)), pl.BlockSpec((bm, 1), lambda i, j: (i, 0)),
                  pl.BlockSpec((None, d, rank), lambda i, j: (layer, 0, 0)),
                  pl.BlockSpec((None, rank, bn), lambda i, j: (layer, 0, j)),
                  pl.BlockSpec((None, bm, pdim), lambda i, j: (layer, i, 0)),
                  pl.BlockSpec((None, pdim, bn), lambda i, j: (layer, 0, j)),
                  pl.BlockSpec((bm, bn), lambda i, j: (i, j)),
                  pl.BlockSpec((1, bn), lambda i, j: (0, j))],
        out_specs=[pl.BlockSpec((bm, bn), lambda i, j: (i, j)),
                   pl.BlockSpec((bm, bn), lambda i, j: (i, j)),
                   pl.BlockSpec((bm, 1), lambda i, j: (i, 0))],
        out_shape=[jax.ShapeDtypeStruct((s, d), F32), jax.ShapeDtypeStruct((s, d), BF16),
                   jax.ShapeDtypeStruct((s, 1), F32)],
        scratch_shapes=[pltpu.VMEM((bm, rank), BF16)],
        compiler_params=_params(("parallel", "arbitrary"), 56),
        name="per_layer_input",
    )(hg, ssq, w_down, w_up, p, w_ple, h, g_next.reshape(1, d))


def kernel(x, p, norm_mix, w_in, b_f, pool_w, pool_scale, conv_w, conv_b, conv_ln_g, conv_ln_b, sgu_ln_g, sgu_ln_b, sgu_w, sgu_b, w_branch, w_gate_down, w_gate_up, b_gate, w_out, norm_ffn, w_ffn_gate, w_ffn_up, w_ffn_down, norm_ple, w_ple_gate_down, w_ple_gate_up, w_ple, norm_final):
    batch, s, d = x.shape
    assert batch == 1
    depth = w_in.shape[0]
    c = pool_scale.shape[1]
    n_heads = b_f.shape[1]
    rank = w_gate_down.shape[2]
    assert c == n_heads * HEAD_DIM and w_in.shape[2] == 8 * c + n_heads
    off_f = 4 * c

    bf = lambda a: a.astype(BF16)
    w_qkv, w_cs = _split_w_in(w_in, off_f, off_f + n_heads)
    f_cols = jnp.pad(w_in[:, :, off_f:off_f + n_heads], ((0, 0), (0, 0), (0, LANES - n_heads)))
    w_tail = bf(jnp.concatenate([w_gate_down, f_cols], axis=2))
    vec3 = lambda a: a.reshape(depth, 1, a.shape[-1])
    bf_pad = vec3(jnp.pad(b_f, ((0, 0), (0, LANES - n_heads))))
    pool_scale, conv_ln_g, conv_ln_b = vec3(pool_scale), vec3(conv_ln_g), vec3(conv_ln_b)
    kw = conv_w.shape[1]
    conv_w = conv_w.reshape(depth, kw, c // LANES, LANES).transpose(0, 2, 1, 3)
    conv_b = conv_b.reshape(depth, c // LANES, 1, LANES)
    sgu_ln_g, sgu_ln_b, norm_ffn, norm_ple = vec3(sgu_ln_g), vec3(sgu_ln_b), vec3(norm_ffn), vec3(norm_ple)
    ones4c = jnp.ones((1, 4 * c), F32)
    qkv_scale = ones4c.at[:, c:2 * c].set(HEAD_DIM ** -0.5 * LOG2E)
    pool_wb, w_branch_b, w_gate_up_b, w_out_b = bf(pool_w), bf(w_branch), bf(w_gate_up), bf(w_out)
    w_fg, w_fu, w_fd = bf(w_ffn_gate), bf(w_ffn_up), bf(w_ffn_down)
    w_pd, w_pu, w_pl = bf(w_ple_gate_down), bf(w_ple_gate_up), bf(w_ple)
    sgu_bt = sgu_b.transpose(0, 2, 1)
    p3 = p.reshape(depth, s, p.shape[-1])

    h = x.reshape(s, d)
    hg, ssq = _norm_prep(h, norm_mix[0])
    for i in range(depth):
        qkv = _in_proj(hg, ssq, w_qkv, i, qkv_scale)
        cs = _in_proj(hg, ssq, w_cs, i, ones4c)
        z, f_pieces = _gate_tail(hg, ssq, w_tail, i, bf_pad, rank)
        branches = (
            _pool_mixer(qkv, 0, pool_wb, i, pool_scale),
            _fox_attention(qkv, f_pieces, n_heads),
            _conformer_conv(cs, 0, conv_w, conv_b, conv_ln_g, conv_ln_b, i),
            _spatial_gating(cs, 2, sgu_w, sgu_bt, sgu_ln_g, sgu_ln_b, i),
        )
        merged = _merge(branches, z, w_branch_b, w_gate_up_b, b_gate, i)
        h, hg, ssq = _resid_matmul(merged, w_out_b, i, h, norm_ffn, i, 1024, 512)

        act = _ffn_act(hg, ssq, w_fg, w_fu, i)
        h, hg, ssq = _resid_matmul(act, w_fd, i, h, norm_ple, i, 512, 512)

        g_next = norm_mix[i + 1] if i + 1 < depth else norm_final
        h, hg, ssq = _ple(hg, ssq, w_pd, w_pu, p3, w_pl, i, h, g_next)
    return _final_norm(h, ssq, norm_final).reshape(batch, s, d)
```

```python
import functools
import math

import jax
import jax.numpy as jnp
from jax import lax
from jax.experimental import pallas as pl
from jax.experimental.pallas import tpu as pltpu

F32 = jnp.float32
BF16 = jnp.bfloat16

EPS = 1e-6
HEAD_DIM = 128
POOL_WINDOWS = (2, 4, 8, 16)
GELU_C0 = math.sqrt(2.0 / math.pi)
GELU_C1 = 0.044715
LOG2E = math.log2(math.e)

LANES = 128
MXU_DEPTH = 256
MIB = 1024 * 1024

POOL_HALO = 16
CONV_HALO = 32
N_BIAS_PIECES = 3
ATTN_GROUP = 4
EXP2_UNDERFLOW = 152.0
CONV_STRIP = 128
SUBLANES = 8


def _params(semantics, vmem_mib):
    return pltpu.CompilerParams(dimension_semantics=semantics, vmem_limit_bytes=vmem_mib * MIB)


def _blk(n, pref):
    b = min(n, pref)
    while n % b:
        b -= LANES
    assert b > 0
    return b


def _row_scale(ssq, d):
    return lax.rsqrt(ssq / d + EPS)


def _dot(a, b):
    return jnp.dot(a, b, preferred_element_type=F32)


def _split3(x):
    hi = x.astype(BF16)
    rem = x - hi.astype(F32)
    mid = rem.astype(BF16)
    lo = (rem - mid.astype(F32)).astype(BF16)
    return hi, mid, lo


def _prep_kernel(x_ref, g_ref, hg_ref, ssq_ref):
    x = x_ref[...]
    ssq_ref[...] = jnp.sum(x * x, axis=-1, keepdims=True)
    hg_ref[...] = (x * g_ref[...]).astype(BF16)


def _norm_prep(x, g):
    s, d = x.shape
    bm = _blk(s, 256)
    return pl.pallas_call(
        _prep_kernel,
        grid=(s // bm,),
        in_specs=[pl.BlockSpec((bm, d), lambda i: (i, 0)), pl.BlockSpec((1, d), lambda i: (0, 0))],
        out_specs=[pl.BlockSpec((bm, d), lambda i: (i, 0)), pl.BlockSpec((bm, 1), lambda i: (i, 0))],
        out_shape=[jax.ShapeDtypeStruct((s, d), BF16), jax.ShapeDtypeStruct((s, 1), F32)],
        compiler_params=_params(("parallel",), 32),
        name="norm_prep",
    )(x, g.reshape(1, d))


def _final_kernel(h_ref, ssq_ref, g_ref, o_ref, *, d):
    y = h_ref[...] * _row_scale(ssq_ref[...], d)
    o_ref[...] = y * g_ref[...]


def _final_norm(h, ssq, g):
    s, d = h.shape
    bm = _blk(s, 256)
    return pl.pallas_call(
        functools.partial(_final_kernel, d=d),
        grid=(s // bm,),
        in_specs=[pl.BlockSpec((bm, d), lambda i: (i, 0)), pl.BlockSpec((bm, 1), lambda i: (i, 0)),
                  pl.BlockSpec((1, d), lambda i: (0, 0))],
        out_specs=pl.BlockSpec((bm, d), lambda i: (i, 0)),
        out_shape=jax.ShapeDtypeStruct((s, d), F32),
        compiler_params=_params(("parallel",), 32),
        name="final_norm",
    )(h, ssq, g.reshape(1, d))


def _split_w_kernel(a_ref, b_ref, ao_ref, bo_ref):
    ao_ref[...] = a_ref[...].astype(BF16)
    bo_ref[...] = b_ref[...].astype(BF16)


def _split_w_in(w, n, off):
    depth, d, width = w.shape
    assert off + n == width and off % SUBLANES == 0
    wt = jnp.swapaxes(w, 1, 2)
    br = _blk(n, 256)
    out = lambda: pl.BlockSpec((None, br, d), lambda l, r: (l, r, 0))
    return pl.pallas_call(
        _split_w_kernel,
        grid=(depth, n // br),
        in_specs=[pl.BlockSpec((None, br, d), lambda l, r: (l, r, 0)),
                  pl.BlockSpec((pl.Squeezed(), pl.Element(br), pl.Element(d)), lambda l, r: (l, pl.multiple_of(off + r * br, SUBLANES), 0))],
        out_specs=[out(), out()],
        out_shape=[jax.ShapeDtypeStruct((depth, n, d), BF16)] * 2,
        compiler_params=_params(("parallel", "parallel"), 40),
        name="split_w_in",
    )(wt, wt)


def _in_kernel(x_ref, ssq_ref, wt_ref, cs_ref, o_ref, *, d):
    acc = lax.dot_general(x_ref[...], wt_ref[...], (((1,), (1,)), ((), ())), preferred_element_type=F32)
    o_ref[...] = (acc * _row_scale(ssq_ref[...], d) * cs_ref[...]).astype(o_ref.dtype)


def _in_proj(hg, ssq, wt, layer, col_scale):
    s, d = hg.shape
    n = wt.shape[1]
    bm, bn = _blk(s, 1024), _blk(n, 1024)
    return pl.pallas_call(
        functools.partial(_in_kernel, d=d),
        grid=(s // bm, n // bn),
        in_specs=[pl.BlockSpec((bm, d), lambda i, j: (i, 0)), pl.BlockSpec((bm, 1), lambda i, j: (i, 0)),
                  pl.BlockSpec((None, bn, d), lambda i, j: (layer, j, 0)),
                  pl.BlockSpec((1, bn), lambda i, j: (0, j))],
        out_specs=pl.BlockSpec((bm, bn), lambda i, j: (i, j)),
        out_shape=jax.ShapeDtypeStruct((s, n), BF16),
        compiler_params=_params(("parallel", "parallel"), 56),
        name="in_proj",
    )(hg, ssq, wt, col_scale)


def _tail_kernel(x_ref, ssq_ref, w_ref, bf_ref, z_ref, f_ref, carry_ref, *, d, rank, bm):
    i = pl.program_id(0)

    @pl.when(i == 0)
    def _():
        carry_ref[...] = jnp.zeros_like(carry_ref)

    acc = _dot(x_ref[...], w_ref[...]) * _row_scale(ssq_ref[...], d)
    z_ref[...] = acc[:, :rank].astype(BF16)
    logit = acc[:, rank:] + bf_ref[...]
    logf = jnp.minimum(logit, 0.0) - jnp.log1p(jnp.exp(-jnp.abs(logit)))
    row = lax.broadcasted_iota(jnp.int32, (bm, bm), 0)
    col = lax.broadcasted_iota(jnp.int32, (bm, bm), 1)
    tri = (col <= row).astype(BF16)
    hi, mid, lo = _split3(logf)
    cs = _dot(tri, hi) + _dot(tri, mid) + _dot(tri, lo) + carry_ref[...]
    carry_ref[...] = cs[bm - 1:bm, :]
    for n, piece in enumerate(_split3(cs * LOG2E)):
        f_ref[:, n * LANES:(n + 1) * LANES] = piece


def _gate_tail(hg, ssq, w_tail, layer, bf_pad, rank):
    s, d = hg.shape
    n = w_tail.shape[2]
    bm = _blk(s, 512)
    return pl.pallas_call(
        functools.partial(_tail_kernel, d=d, rank=rank, bm=bm),
        grid=(s // bm,),
        in_specs=[pl.BlockSpec((bm, d), lambda i: (i, 0)), pl.BlockSpec((bm, 1), lambda i: (i, 0)),
                  pl.BlockSpec((None, d, n), lambda i: (layer, 0, 0)),
                  pl.BlockSpec((None, 1, LANES), lambda i: (layer, 0, 0))],
        out_specs=[pl.BlockSpec((bm, rank), lambda i: (i, 0)),
                   pl.BlockSpec((bm, N_BIAS_PIECES * LANES), lambda i: (i, 0))],
        out_shape=[jax.ShapeDtypeStruct((s, rank), BF16),
                   jax.ShapeDtypeStruct((s, N_BIAS_PIECES * LANES), BF16)],
        scratch_shapes=[pltpu.VMEM((1, LANES), F32)],
        compiler_params=_params(("arbitrary",), 40),
        name="gate_tail",
    )(hg, ssq, w_tail, bf_pad)


def _pool_kernel(x_ref, halo_ref, w_ref, sc_ref, o_ref, *, bm, gw):
    i = pl.program_id(0)
    x = x_ref[...].astype(F32)
    halo = jnp.where(i > 0, halo_ref[...].astype(F32), 0.0)
    xx = jnp.concatenate([halo, x], axis=0)
    pos = i * bm + lax.broadcasted_iota(jnp.int32, (bm, 1), 0)
    for g, w in enumerate(POOL_WINDOWS):
        sl = slice(g * gw, (g + 1) * gw)
        s = xx[:, sl]
        span = 1
        while span < w:
            s = s + pltpu.roll(s, span, axis=0)
            span *= 2
        cnt = jnp.minimum(pos + 1, w).astype(F32)
        pooled = s[POOL_HALO:, :] / cnt - x[:, sl]
        mixed = _dot(pooled.astype(BF16), w_ref[g])
        o_ref[:, sl] = (mixed * sc_ref[:, sl]).astype(BF16)


def _pool_mixer(proj, col_blk, pool_w, layer, pool_scale):
    s = proj.shape[0]
    _, ng, gw, _ = pool_w.shape
    c = ng * gw
    assert all(w & (w - 1) == 0 and w - 1 <= POOL_HALO for w in POOL_WINDOWS) and ng == len(POOL_WINDOWS)
    bm = _blk(s, 512)
    per = bm // POOL_HALO
    return pl.pallas_call(
        functools.partial(_pool_kernel, bm=bm, gw=gw),
        grid=(s // bm,),
        in_specs=[pl.BlockSpec((bm, c), lambda i: (i, col_blk)),
                  pl.BlockSpec((POOL_HALO, c), lambda i: (jnp.maximum(i * per - 1, 0), col_blk)),
                  pl.BlockSpec((None, ng, gw, gw), lambda i: (layer, 0, 0, 0)),
                  pl.BlockSpec((None, 1, c), lambda i: (layer, 0, 0))],
        out_specs=pl.BlockSpec((bm, c), lambda i: (i, 0)),
        out_shape=jax.ShapeDtypeStruct((s, c), BF16),
        compiler_params=_params(("parallel",), 32),
        name="pool_mixer",
    )(proj, proj, pool_w, pool_scale)


def _attn_kernel(first_ref, qt_ref, k_ref, vt_ref, o_ref, st0, st1, p0, p1, acc_ref, *, blk):
    i = pl.program_id(1)
    first = first_ref[pl.program_id(0) * pl.num_programs(1) + i]
    qt = qt_ref[...]

    def scores(j):
        off = pl.multiple_of(j * blk, blk)
        return _dot(k_ref[pl.ds(off, blk), :], qt)

    def softmax(st_ref, p_ref, m, l, masked):
        st = st_ref[...]
        if masked:
            key = lax.broadcasted_iota(jnp.int32, (blk, blk), 0)
            qry = lax.broadcasted_iota(jnp.int32, (blk, blk), 1)
            st = jnp.where(key <= qry, st, -jnp.inf)
        m_new = jnp.maximum(m, jnp.max(st, axis=0, keepdims=True))
        alpha = jnp.exp2(m - m_new)
        p = jnp.exp2(st - m_new)
        p_ref[...] = p.astype(BF16)
        return m_new, alpha * l + jnp.sum(p, axis=0, keepdims=True), alpha

    def pv(p_ref, j):
        return _dot(vt_ref[j], p_ref[...])

    st_bufs, p_bufs = (st0, st1), (p0, p1)

    def group(a, m, l, n, tail):
        for b in range(n):
            last = tail and b == n - 1
            if not last:
                st_bufs[(b + 1) % 2][...] = scores(a + b + 1)
            owed = pv(p_bufs[(b + 1) % 2], jnp.maximum(a + b - 1, 0))
            m, l, alpha = softmax(st_bufs[b % 2], p_bufs[b % 2], m, l, last)
            acc_ref[...] = (acc_ref[...] + owed) * alpha
        return m, l

    p1[...] = jnp.zeros_like(p1)
    acc_ref[...] = jnp.zeros_like(acc_ref)
    st0[...] = scores(first)
    m0 = jnp.full((1, blk), -jnp.inf, F32)
    l0 = jnp.zeros((1, blk), F32)
    trips = (i - first) // ATTN_GROUP
    m, l = lax.fori_loop(0, trips, lambda t, c: group(first + ATTN_GROUP * t, c[0], c[1], ATTN_GROUP, False),
                         (m0, l0))
    rem = i - first + 1 - ATTN_GROUP * trips

    for n in range(1, ATTN_GROUP + 1):
        @pl.when(rem == n)
        def _(n=n):
            _, l_fin = group(i + 1 - n, m, l, n, True)
            out = (acc_ref[...] + pv(p_bufs[(n - 1) % 2], i)) / l_fin
            o_ref[...] = out.T.astype(o_ref.dtype)


def _fox_attention(proj, f_pieces, n_heads):
    s = proj.shape[0]
    c = n_heads * HEAD_DIM
    blk = _blk(s, 512)
    nb = s // blk
    heads = lambda a: a.reshape(s, n_heads, HEAD_DIM)
    pieces = f_pieces.reshape(s, N_BIAS_PIECES, LANES)[:, :, :n_heads].transpose(0, 2, 1)
    ones = jnp.ones_like(pieces)
    zpad = jnp.zeros((s, n_heads, MXU_DEPTH - HEAD_DIM - 2 * N_BIAS_PIECES), BF16)
    q_aug = jnp.concatenate([heads(proj[:, c:2 * c]), pieces, ones, zpad], axis=-1)
    k_aug = jnp.concatenate([heads(proj[:, 2 * c:3 * c]), ones, -pieces, zpad], axis=-1)
    qt = q_aug.transpose(1, 2, 0)
    ka = k_aug.transpose(1, 0, 2)
    vt = proj[:, 3 * c:4 * c].reshape(nb, blk, n_heads, HEAD_DIM).transpose(2, 0, 3, 1)
    first = _first_contributing_block(q_aug, k_aug, pieces, nb, blk)
    grid_spec = pltpu.PrefetchScalarGridSpec(
        num_scalar_prefetch=1,
        grid=(n_heads, nb),
        in_specs=[pl.BlockSpec((None, MXU_DEPTH, blk), lambda h, i, first: (h, 0, i)),
                  pl.BlockSpec((None, s, MXU_DEPTH), lambda h, i, first: (h, 0, 0)),
                  pl.BlockSpec((None, nb, HEAD_DIM, blk), lambda h, i, first: (h, 0, 0, 0))],
        out_specs=pl.BlockSpec((blk, HEAD_DIM), lambda h, i, first: (i, h)),
        scratch_shapes=[pltpu.VMEM((blk, blk), F32), pltpu.VMEM((blk, blk), F32),
                        pltpu.VMEM((blk, blk), BF16), pltpu.VMEM((blk, blk), BF16),
                        pltpu.VMEM((HEAD_DIM, blk), F32)])
    return pl.pallas_call(
        functools.partial(_attn_kernel, blk=blk),
        grid_spec=grid_spec,
        out_shape=jax.ShapeDtypeStruct((s, c), BF16),
        compiler_params=_params(("parallel", "parallel"), 48),
        name="fox_attention",
    )(first, qt, ka, vt)


def _first_contributing_block(q_aug, k_aug, pieces, nb, blk):
    s, n_heads, _ = q_aug.shape
    norm = lambda a: jnp.sqrt(jnp.max(jnp.sum(jnp.square(a[:, :, :HEAD_DIM].astype(F32)), axis=-1), axis=0))
    slack = 2.0 * norm(q_aug) * norm(k_aug) * (1.0 + 2.0 ** -6) + 2.0
    f = jnp.sum(pieces.astype(F32), axis=-1).reshape(nb, blk, n_heads)
    f_max, f_min = jnp.max(f, axis=1), jnp.min(f, axis=1)
    bound = f_max[:, None, :] - f_min[None, :, :] + slack
    needed = bound >= -EXP2_UNDERFLOW
    needed = needed | (jnp.arange(nb)[:, None] == jnp.arange(nb)[None, :])[:, :, None]
    first = jnp.argmax(needed, axis=1).astype(jnp.int32)
    return first.T.reshape(n_heads * nb)


def _layer_norm(x, g, b):
    mu = jnp.mean(x, axis=-1, keepdims=True)
    xc = x - mu
    var = jnp.mean(xc * xc, axis=-1, keepdims=True)
    return xc * lax.rsqrt(var + EPS) * g + b


def _conv_kernel(a_ref, g_ref, ah_ref, gh_ref, cw_ref, cb_ref, lg_ref, lb_ref, o_ref, u_scr, y_scr, *, bm, kw):
    i = pl.program_id(0)
    n_tiles = cw_ref.shape[0]
    u = a_ref[...].astype(F32) * jax.nn.sigmoid(g_ref[...].astype(F32))
    uh = ah_ref[...].astype(F32) * jax.nn.sigmoid(gh_ref[...].astype(F32))
    uh = jnp.where(i > 0, uh, 0.0)
    n_shifted = bm + CONV_HALO - SUBLANES
    for t in range(n_tiles):
        cs = slice(t * LANES, (t + 1) * LANES)
        u_scr[0, t, :CONV_HALO, :] = uh[:, cs]
        u_scr[0, t, CONV_HALO:, :] = u[:, cs]
        for r in range(1, SUBLANES):
            u_scr[r, t, :n_shifted, :] = u_scr[0, t, pl.ds(r, n_shifted), :]

    def tile(t, carry):
        for r0 in range(0, bm, CONV_STRIP):
            y = jnp.zeros((CONV_STRIP, LANES), F32) + cb_ref[t]
            for j in range(kw):
                q, r = divmod(CONV_HALO - (kw - 1) + j, SUBLANES)
                y = y + cw_ref[t, j:j + 1, :] * u_scr[r, t, r0 + SUBLANES * q:r0 + SUBLANES * q + CONV_STRIP, :]
            y_scr[t, r0:r0 + CONV_STRIP, :] = y
        return carry

    lax.fori_loop(0, n_tiles, tile, 0)
    y = jnp.concatenate([y_scr[t] for t in range(n_tiles)], axis=1)
    y = _layer_norm(y, lg_ref[...], lb_ref[...])
    o_ref[...] = (y * jax.nn.sigmoid(y)).astype(BF16)


def _conformer_conv(proj, a_blk, conv_w, conv_b, ln_g, ln_b, layer):
    s = proj.shape[0]
    _, n_tiles, kw, _ = conv_w.shape
    c = n_tiles * LANES
    assert kw - 1 <= CONV_HALO
    bm = _blk(s, 256)
    assert bm % CONV_STRIP == 0
    per = bm // CONV_HALO
    halo_row = lambda i: jnp.maximum(i * per - 1, 0)
    vec = lambda: pl.BlockSpec((None, 1, c), lambda i: (layer, 0, 0))
    return pl.pallas_call(
        functools.partial(_conv_kernel, bm=bm, kw=kw),
        grid=(s // bm,),
        in_specs=[pl.BlockSpec((bm, c), lambda i: (i, a_blk)),
                  pl.BlockSpec((bm, c), lambda i: (i, a_blk + 1)),
                  pl.BlockSpec((CONV_HALO, c), lambda i: (halo_row(i), a_blk)),
                  pl.BlockSpec((CONV_HALO, c), lambda i: (halo_row(i), a_blk + 1)),
                  pl.BlockSpec((None, n_tiles, kw, LANES), lambda i: (layer, 0, 0, 0)),
                  pl.BlockSpec((None, n_tiles, 1, LANES), lambda i: (layer, 0, 0, 0)), vec(), vec()],
        out_specs=pl.BlockSpec((bm, c), lambda i: (i, 0)),
        out_shape=jax.ShapeDtypeStruct((s, c), BF16),
        scratch_shapes=[pltpu.VMEM((SUBLANES, n_tiles, bm + CONV_HALO, LANES), F32),
                        pltpu.VMEM((n_tiles, bm, LANES), F32)],
        compiler_params=_params(("parallel",), 32),
        name="conformer_conv",
    )(proj, proj, proj, proj, conv_w, conv_b, ln_g, ln_b)


def _gelu_tanh(x):
    return x * (0.5 * (1.0 + jnp.tanh(GELU_C0 * (x + GELU_C1 * (x * x * x)))))


def _sgu_kernel(u_ref, v_ref, w_ref, bt_ref, lg_ref, lb_ref, o_ref, *, bm, chunk, gw):
    u = _gelu_tanh(u_ref[...].astype(F32))
    v = _layer_norm(_gelu_tanh(v_ref[...].astype(F32)), lg_ref[...], lb_ref[...]).astype(BF16)
    row = lax.broadcasted_iota(jnp.int32, (chunk, chunk), 0)
    col = lax.broadcasted_iota(jnp.int32, (chunk, chunk), 1)
    for g in range(w_ref.shape[0]):
        wg = jnp.where(col <= row, w_ref[g], 0.0).astype(BF16)
        bias = bt_ref[:, g:g + 1]
        cs = slice(g * gw, (g + 1) * gw)
        for n in range(bm // chunk):
            rs = slice(n * chunk, (n + 1) * chunk)
            mixed = _dot(wg, v[rs, cs]) + bias
            o_ref[rs, cs] = (u[rs, cs] * mixed).astype(BF16)


def _spatial_gating(proj, u_blk, w_s, b_st, ln_g, ln_b, layer):
    s = proj.shape[0]
    _, ng, chunk, _ = w_s.shape
    c = ln_g.shape[-1]
    gw = c // ng
    bm = _blk(s, 4 * chunk)
    vec = lambda: pl.BlockSpec((None, 1, c), lambda i: (layer, 0, 0))
    return pl.pallas_call(
        functools.partial(_sgu_kernel, bm=bm, chunk=chunk, gw=gw),
        grid=(s // bm,),
        in_specs=[pl.BlockSpec((bm, c), lambda i: (i, u_blk)),
                  pl.BlockSpec((bm, c), lambda i: (i, u_blk + 1)),
                  pl.BlockSpec((None, ng, chunk, chunk), lambda i: (layer, 0, 0, 0)),
                  pl.BlockSpec((None, chunk, ng), lambda i: (layer, 0, 0)), vec(), vec()],
        out_specs=pl.BlockSpec((bm, c), lambda i: (i, 0)),
        out_shape=jax.ShapeDtypeStruct((s, c), BF16),
        compiler_params=_params(("parallel",), 32),
        name="spatial_gating",
    )(proj, proj, w_s, b_st, ln_g, ln_b)


def _merge_kernel(b0_ref, b1_ref, b2_ref, b3_ref, z_ref, wb_ref, wg_ref, bg_ref, o_ref):
    z = z_ref[...]
    merged = None
    for b, br in enumerate((b0_ref, b1_ref, b2_ref, b3_ref)):
        gate = jax.nn.sigmoid(_dot(z, wg_ref[b]) + bg_ref[b:b + 1, :])
        term = gate * _dot(br[...], wb_ref[b])
        merged = term if merged is None else merged + term
    o_ref[...] = merged.astype(BF16)


def _merge(branches, z, w_branch, w_gate_up, b_gate, layer):
    s, c = branches[0].shape
    _, nb, rank, d = w_gate_up.shape
    assert nb == 4 and len(branches) == 4
    bm, bn = _blk(s, 512), _blk(d, 1024)
    act = lambda w: pl.BlockSpec((bm, w), lambda j, i: (i, 0))
    return pl.pallas_call(
        _merge_kernel,
        grid=(d // bn, s // bm),
        in_specs=[act(c), act(c), act(c), act(c), act(rank),
                  pl.BlockSpec((None, nb, c, bn), lambda j, i: (layer, 0, 0, j)),
                  pl.BlockSpec((None, nb, rank, bn), lambda j, i: (layer, 0, 0, j)),
                  pl.BlockSpec((None, nb, bn), lambda j, i: (layer, 0, j))],
        out_specs=pl.BlockSpec((bm, bn), lambda j, i: (i, j)),
        out_shape=jax.ShapeDtypeStruct((s, d), BF16),
        compiler_params=_params(("parallel", "parallel"), 56),
        name="branch_merge",
    )(*branches, z, w_branch, w_gate_up, b_gate)


def _emit_resid(hn, j, g_ref, ho_ref, hg_ref, ssq_ref):
    ho_ref[...] = hn
    hg_ref[...] = (hn * g_ref[...]).astype(BF16)
    part = jnp.sum(hn * hn, axis=-1, keepdims=True)

    @pl.when(j == 0)
    def _():
        ssq_ref[...] = part

    @pl.when(j > 0)
    def _():
        ssq_ref[...] += part


def _resid_kernel(x_ref, w_ref, h_ref, g_ref, ho_ref, hg_ref, ssq_ref):
    hn = h_ref[...] + _dot(x_ref[...], w_ref[...])
    _emit_resid(hn, pl.program_id(1), g_ref, ho_ref, hg_ref, ssq_ref)


def _resid_matmul(x, w, layer, h, g, g_layer, bm_pref, bn_pref):
    s, kdim = x.shape
    d = w.shape[2]
    bm, bn = _blk(s, bm_pref), _blk(d, bn_pref)
    return pl.pallas_call(
        _resid_kernel,
        grid=(s // bm, d // bn),
        in_specs=[pl.BlockSpec((bm, kdim), lambda i, j: (i, 0)),
                  pl.BlockSpec((None, kdim, bn), lambda i, j: (layer, 0, j)),
                  pl.BlockSpec((bm, bn), lambda i, j: (i, j)),
                  pl.BlockSpec((None, 1, bn), lambda i, j: (g_layer, 0, j))],
        out_specs=[pl.BlockSpec((bm, bn), lambda i, j: (i, j)),
                   pl.BlockSpec((bm, bn), lambda i, j: (i, j)),
                   pl.BlockSpec((bm, 1), lambda i, j: (i, 0))],
        out_shape=[jax.ShapeDtypeStruct((s, d), F32), jax.ShapeDtypeStruct((s, d), BF16),
                   jax.ShapeDtypeStruct((s, 1), F32)],
        compiler_params=_params(("parallel", "arbitrary"), 56),
        name="resid_matmul",
    )(x, w, h, g)


def _ffn_act_kernel(x_ref, ssq_ref, wg_ref, wu_ref, o_ref, *, d):
    r = _row_scale(ssq_ref[...], d)
    x = x_ref[...]
    a = _dot(x, wg_ref[...]) * r
    u = _dot(x, wu_ref[...]) * r
    o_ref[...] = (a * jax.nn.sigmoid(a) * u).astype(BF16)


def _ffn_act(hg, ssq, w_gate, w_up, layer):
    s, d = hg.shape
    f = w_gate.shape[2]
    bm, bn = _blk(s, 1024), _blk(f, 512)
    wspec = lambda: pl.BlockSpec((None, d, bn), lambda i, j: (layer, 0, j))
    return pl.pallas_call(
        functools.partial(_ffn_act_kernel, d=d),
        grid=(s // bm, f // bn),
        in_specs=[pl.BlockSpec((bm, d), lambda i, j: (i, 0)), pl.BlockSpec((bm, 1), lambda i, j: (i, 0)),
                  wspec(), wspec()],
        out_specs=pl.BlockSpec((bm, bn), lambda i, j: (i, j)),
        out_shape=jax.ShapeDtypeStruct((s, f), BF16),
        compiler_params=_params(("parallel", "parallel"), 56),
        name="ffn_act",
    )(hg, ssq, w_gate, w_up)


def _ple_kernel(hg_ref, ssq_ref, wd_ref, wu_ref, p_ref, wp_ref, h_ref, g_ref,
                ho_ref, xg_ref, ssqo_ref, t_ref, *, d):
    j = pl.program_id(1)

    @pl.when(j == 0)
    def _():
        t = _dot(hg_ref[...], wd_ref[...]) * _row_scale(ssq_ref[...], d)
        t_ref[...] = t.astype(BF16)

    gate = jax.nn.sigmoid(_dot(t_ref[...], wu_ref[...]))
    inj = _dot(p_ref[...].astype(BF16), wp_ref[...])
    _emit_resid(h_ref[...] + gate * inj, j, g_ref, ho_ref, xg_ref, ssqo_ref)


def _ple(hg, ssq, w_down, w_up, p, w_ple, layer, h, g_next):
    s, d = hg.shape
    rank = w_down.shape[2]
    pdim = p.shape[2]
    bm, bn = _blk(s, 1024), _blk(d, 1024)
    return pl.pallas_call(
        functools.partial(_ple_kernel, d=d),
        grid=(s // bm, d // bn),
        in_specs=[pl.BlockSpec((bm, d), lambda i, j: (i, 0)), pl.BlockSpec((bm, 1), lambda i, j: (i, 0)),
                  pl.BlockSpec((None, d, rank), lambda i, j: (layer, 0, 0)),
                  pl.BlockSpec((None, rank, bn), lambda i, j: (layer, 0, j)),
                  pl.BlockSpec((None, bm, pdim), lambda i, j: (layer, i, 0)),
                  pl.BlockSpec((None, pdim, bn), lambda i, j: (layer, 0, j)),
                  pl.BlockSpec((bm, bn), lambda i, j: (i, j)),
                  pl.BlockSpec((1, bn), lambda i, j: (0, j))],
        out_specs=[pl.BlockSpec((bm, bn), lambda i, j: (i, j)),
                   pl.BlockSpec((bm, bn), lambda i, j: (i, j)),
                   pl.BlockSpec((bm, 1), lambda i, j: (i, 0))],
        out_shape=[jax.ShapeDtypeStruct((s, d), F32), jax.ShapeDtypeStruct((s, d), BF16),
                   jax.ShapeDtypeStruct((s, 1), F32)],
        scratch_shapes=[pltpu.VMEM((bm, rank), BF16)],
        compiler_params=_params(("parallel", "arbitrary"), 56),
        name="per_layer_input",
    )(hg, ssq, w_down, w_up, p, w_ple, h, g_next.reshape(1, d))


def kernel(x, p, norm_mix, w_in, b_f, pool_w, pool_scale, conv_w, conv_b, conv_ln_g, conv_ln_b, sgu_ln_g, sgu_ln_b, sgu_w, sgu_b, w_branch, w_gate_down, w_gate_up, b_gate, w_out, norm_ffn, w_ffn_gate, w_ffn_up, w_ffn_down, norm_ple, w_ple_gate_down, w_ple_gate_up, w_ple, norm_final):
    batch, s, d = x.shape
    assert batch == 1
    depth = w_in.shape[0]
    c = pool_scale.shape[1]
    n_heads = b_f.shape[1]
    rank = w_gate_down.shape[2]
    assert c == n_heads * HEAD_DIM and w_in.shape[2] == 8 * c + n_heads
    off_f = 4 * c

    bf = lambda a: a.astype(BF16)
    w_qkv, w_cs = _split_w_in(w_in, off_f, off_f + n_heads)
    f_cols = jnp.pad(w_in[:, :, off_f:off_f + n_heads], ((0, 0), (0, 0), (0, LANES - n_heads)))
    w_tail = bf(jnp.concatenate([w_gate_down, f_cols], axis=2))
    vec3 = lambda a: a.reshape(depth, 1, a.shape[-1])
    bf_pad = vec3(jnp.pad(b_f, ((0, 0), (0, LANES - n_heads))))
    pool_scale, conv_ln_g, conv_ln_b = vec3(pool_scale), vec3(conv_ln_g), vec3(conv_ln_b)
    kw = conv_w.shape[1]
    conv_w = conv_w.reshape(depth, kw, c // LANES, LANES).transpose(0, 2, 1, 3)
    conv_b = conv_b.reshape(depth, c // LANES, 1, LANES)
    sgu_ln_g, sgu_ln_b, norm_ffn, norm_ple = vec3(sgu_ln_g), vec3(sgu_ln_b), vec3(norm_ffn), vec3(norm_ple)
    ones4c = jnp.ones((1, 4 * c), F32)
    qkv_scale = ones4c.at[:, c:2 * c].set(HEAD_DIM ** -0.5 * LOG2E)
    pool_wb, w_branch_b, w_gate_up_b, w_out_b = bf(pool_w), bf(w_branch), bf(w_gate_up), bf(w_out)
    w_fg, w_fu, w_fd = bf(w_ffn_gate), bf(w_ffn_up), bf(w_ffn_down)
    w_pd, w_pu, w_pl = bf(w_ple_gate_down), bf(w_ple_gate_up), bf(w_ple)
    sgu_bt = sgu_b.transpose(0, 2, 1)
    p3 = p.reshape(depth, s, p.shape[-1])

    h = x.reshape(s, d)
    hg, ssq = _norm_prep(h, norm_mix[0])
    for i in range(depth):
        qkv = _in_proj(hg, ssq, w_qkv, i, qkv_scale)
        cs = _in_proj(hg, ssq, w_cs, i, ones4c)
        z, f_pieces = _gate_tail(hg, ssq, w_tail, i, bf_pad, rank)
        branches = (
            _pool_mixer(qkv, 0, pool_wb, i, pool_scale),
            _fox_attention(qkv, f_pieces, n_heads),
            _conformer_conv(cs, 0, conv_w, conv_b, conv_ln_g, conv_ln_b, i),
            _spatial_gating(cs, 2, sgu_w, sgu_bt, sgu_ln_g, sgu_ln_b, i),
        )
        merged = _merge(branches, z, w_branch_b, w_gate_up_b, b_gate, i)
        h, hg, ssq = _resid_matmul(merged, w_out_b, i, h, norm_ffn, i, 1024, 512)

        act = _ffn_act(hg, ssq, w_fg, w_fu, i)
        h, hg, ssq = _resid_matmul(act, w_fd, i, h, norm_ple, i, 512, 512)

        g_next = norm_mix[i + 1] if i + 1 < depth else norm_final
        h, hg, ssq = _ple(hg, ssq, w_pd, w_pu, p3, w_pl, i, h, g_next)
    return _final_norm(h, ssq, norm_final).reshape(batch, s, d)
```
